```python
import math
import jax, jax.numpy as jnp
from jax import lax
import numpy as np

D_MODEL = 4096
BATCH = 2
SEQ = 4096
DEPTH = 1
DEC_BATCH = 32
DEC_SEQ = 1
PAST_LEN = 8192
PAGE_SIZE = 128

H_A = 16
DK_A = 128
DV_A = 128
CONV_W = 4
CONV_DIM = H_A * (2 * DK_A + DV_A)
CHUNK = 64
H_B = 8
DH_B = 128
GROUPS = ((128, 1), (512, 4), (2048, 16))
N_GROUPS = len(GROUPS)
H_ATTN = N_GROUPS * H_B
Q_BLOCK = 128
NUM_BUCKETS = 32
MAX_DISTANCE = 2048
N_EXPERTS = 32
TOP_K = 4
D_FF = D_MODEL
SWIGLU_LIMIT = 7.0
SWIGLU_ALPHA = 1.702
MOE_BLOCK = 128
DN_ALPHA = (2.0 * DEPTH) ** 0.25
DN_BETA = (8.0 * DEPTH) ** -0.25
LN_EPS = 1e-5
RMS_EPS = 1e-6

OFF_Z = CONV_DIM
OFF_BETA = OFF_Z + H_A * DV_A
OFF_ALPHA = OFF_BETA + H_A
OFF_B = OFF_ALPHA + H_A
OFF_GATE = OFF_B + N_GROUPS * 3 * H_B * DH_B
N_IN = OFF_GATE + 2 * D_MODEL

kernel_name = 'gdn_dilated_swa_moe_step'

F32 = jnp.float32


def layer_norm(x, g, b):
    xf = x.astype(F32)
    mu = jnp.mean(xf, -1, keepdims=True)
    var = jnp.mean(jnp.square(xf - mu), -1, keepdims=True)
    return ((xf - mu) * lax.rsqrt(var + LN_EPS) * g.astype(F32) + b.astype(F32)).astype(x.dtype)


def l2norm(t):
    return t * lax.rsqrt(jnp.sum(t * t, -1, keepdims=True) + RMS_EPS)


def t5_bucket(dist):
    max_exact = NUM_BUCKETS // 2
    d = jnp.maximum(dist, 1).astype(F32)
    large = max_exact + (jnp.log(d / max_exact) / math.log(MAX_DISTANCE / max_exact)
                         * (NUM_BUCKETS - max_exact)).astype(jnp.int32)
    large = jnp.minimum(large, NUM_BUCKETS - 1)
    return jnp.where(dist < max_exact, dist, large)


def in_proj(x, w_in, b_gate):
    b, s, _ = x.shape
    p = jnp.einsum('bsd,dn->bsn', x, w_in)
    qkv_a = p[..., :OFF_Z]
    z = p[..., OFF_Z:OFF_BETA]
    beta_logit = p[..., OFF_BETA:OFF_ALPHA]
    alpha_in = p[..., OFF_ALPHA:OFF_B]
    qkv_b = p[..., OFF_B:OFF_GATE].reshape(b, s, N_GROUPS, 3, H_B, DH_B)
    gates = jax.nn.sigmoid((p[..., OFF_GATE:] + b_gate).astype(F32)).astype(x.dtype)
    return qkv_a, z, beta_logit, alpha_in, qkv_b, gates.reshape(b, s, 2, D_MODEL)


def causal_conv_silu(xh, w):
    t = xh.shape[1] - (CONV_W - 1)
    acc = xh[:, 0:t] * w[0]
    for i in range(1, CONV_W):
        acc = acc + xh[:, i:i + t] * w[i]
    return jax.nn.silu(acc)


def delta_inputs(qkv, beta_logit, alpha_in, a_log, dt_bias):
    b, t, _ = qkv.shape
    f = qkv.astype(F32)
    q = l2norm(f[..., :H_A * DK_A].reshape(b, t, H_A, DK_A)) * (DK_A ** -0.5)
    k = l2norm(f[..., H_A * DK_A:2 * H_A * DK_A].reshape(b, t, H_A, DK_A))
    v = f[..., 2 * H_A * DK_A:].reshape(b, t, H_A, DV_A)
    beta = jax.nn.sigmoid(beta_logit.astype(F32))
    g = -jnp.exp(a_log.astype(F32)) * jax.nn.softplus(alpha_in.astype(F32) + dt_bias.astype(F32))
    return q, k, v, beta, g


def gated_delta_chunked(q, k, v, beta, g, s0):
    b, t, h, _ = q.shape
    n = t // CHUNK

    def chunks(a):
        return a.reshape(b, n, CHUNK, h, -1).transpose(1, 0, 3, 2, 4)

    qc, kc, vc = chunks(q), chunks(k), chunks(v)
    bc = beta.reshape(b, n, CHUNK, h).transpose(1, 0, 3, 2)
    gc = lax.cumsum(g.reshape(b, n, CHUNK, h).transpose(1, 0, 3, 2), axis=3)
    tril = jnp.tril(jnp.ones((CHUNK, CHUNK), bool))
    strict = jnp.tril(jnp.ones((CHUNK, CHUNK), bool), k=-1)
    diff = gc[..., :, None] - gc[..., None, :]
    decay = jnp.where(tril, jnp.exp(jnp.where(tril, diff, 0.0)), 0.0)
    kb = kc * bc[..., None]
    m = jnp.where(strict, jnp.einsum('nbhik,nbhjk->nbhij', kb, kc) * decay, 0.0)
    rhs = jnp.concatenate([vc * bc[..., None], kb * jnp.exp(gc)[..., None]], axis=-1)
    sol = lax.linalg.triangular_solve(jnp.eye(CHUNK, dtype=F32) + m, rhs, left_side=True, lower=True)
    u, w = sol[..., :DV_A], sol[..., DV_A:]
    attn = jnp.where(tril, jnp.einsum('nbhik,nbhjk->nbhij', qc, kc) * decay, 0.0)
    qg = qc * jnp.exp(gc)[..., None]
    kd = kc * jnp.exp(gc[..., -1:] - gc)[..., None]
    glast = jnp.exp(gc[..., -1])

    def step(s, xs):
        u_i, w_i, attn_i, qg_i, kd_i, gl_i = xs
        v_new = u_i - jnp.einsum('bhck,bhkv->bhcv', w_i, s)
        o = jnp.einsum('bhck,bhkv->bhcv', qg_i, s) + jnp.einsum('bhij,bhjv->bhiv', attn_i, v_new)
        s = s * gl_i[..., None, None] + jnp.einsum('bhck,bhcv->bhkv', kd_i, v_new)
        return s, o

    s, o = lax.scan(step, s0, (u, w, attn, qg, kd, glast))
    return o.transpose(1, 0, 3, 2, 4).reshape(b, t, h, DV_A), s


def gated_delta_recurrent(q, k, v, beta, g, s0):
    def step(s, xs):
        q_t, k_t, v_t, b_t, g_t = xs
        s = s * jnp.exp(g_t)[..., None, None]
        kv = jnp.einsum('bhkv,bhk->bhv', s, k_t)
        s = s + jnp.einsum('bhk,bhv->bhkv', k_t, (v_t - kv) * b_t[..., None])
        return s, jnp.einsum('bhkv,bhk->bhv', s, q_t)

    s, o = lax.scan(step, s0, tuple(a.swapaxes(0, 1) for a in (q, k, v, beta, g)))
    return o.swapaxes(0, 1), s


def delta_out(o, z, norm_w, dtype):
    b, t = o.shape[:2]
    o = o * lax.rsqrt(jnp.mean(o * o, -1, keepdims=True) + RMS_EPS) * norm_w.astype(F32)
    o = o * jax.nn.silu(z.reshape(b, t, H_A, DV_A).astype(F32))
    return o.reshape(b, t, H_A * DV_A).astype(dtype)


def softmax_attend(logits, v, eq):
    m = jnp.max(logits, -1, keepdims=True)
    e = jnp.exp(logits - m)
    s = jnp.sum(e, -1, keepdims=True)
    o = jnp.einsum(eq, (e / s).astype(v.dtype), v)
    return o, (m + jnp.log(s))[..., 0]


def dilated_attn_prompt(q, k, v, dil, steps, bias_g):
    b, s, h, dh = q.shape
    L = s // dil
    nb = -(-L // Q_BLOCK)
    lp = nb * Q_BLOCK

    def by_residue(a):
        return a.reshape(b, L, dil, h, dh).transpose(0, 2, 1, 3, 4)

    qr = jnp.pad(by_residue(q), ((0, 0), (0, 0), (0, lp - L), (0, 0), (0, 0))).reshape(b, dil, nb, Q_BLOCK, h, dh)

    def key_window(a):
        ap = jnp.pad(by_residue(a), ((0, 0), (0, 0), (Q_BLOCK, lp - L), (0, 0), (0, 0)))
        prev = ap[:, :, :lp].reshape(b, dil, nb, Q_BLOCK, h, dh)
        cur = ap[:, :, Q_BLOCK:].reshape(b, dil, nb, Q_BLOCK, h, dh)
        return jnp.concatenate([prev, cur], axis=3)

    kw, vw = key_window(k), key_window(v)
    qi = jnp.arange(Q_BLOCK)[:, None]
    kj = jnp.arange(2 * Q_BLOCK)[None, :]
    rel = qi + Q_BLOCK - kj
    blk_start = (jnp.arange(nb) * Q_BLOCK - Q_BLOCK)[:, None, None]
    valid = (rel >= 0) & (rel <= steps) & (blk_start + kj >= 0)
    bias = bias_g[t5_bucket(jnp.maximum(rel, 0) * dil)].astype(F32).transpose(2, 0, 1)
    logits = jnp.einsum('brnqhd,brnkhd->brnhqk', qr, kw).astype(F32) * (dh ** -0.5) + bias
    logits = jnp.where(valid[:, None], logits, -jnp.inf)
    o, lse = softmax_attend(logits, vw, 'brnhqk,brnkhd->brnqhd')
    o = o.reshape(b, dil, lp, h, dh)[:, :, :L].transpose(0, 2, 1, 3, 4).reshape(b, s, h, dh)
    lse = lse.transpose(0, 1, 2, 4, 3).reshape(b, dil, lp, h)[:, :, :L].transpose(0, 2, 1, 3).reshape(b, s, h)
    return o, lse


def dilated_attn_sample(q, k_new, v_new, kv_buf, dil, steps, bias_g):
    n, t, h, dh = q.shape
    wb = kv_buf.shape[1]
    kc = jnp.concatenate([kv_buf[:, :, 0].astype(k_new.dtype), k_new], axis=1)
    vc = jnp.concatenate([kv_buf[:, :, 1].astype(v_new.dtype), v_new], axis=1)
    j = jnp.arange(steps + 1)
    idx = wb + jnp.arange(t)[:, None] - j[None, :] * dil
    valid = idx >= 0
    idx = jnp.maximum(idx, 0)
    kg, vg = kc[:, idx], vc[:, idx]
    bias = bias_g[t5_bucket(j * dil)].astype(F32).T
    logits = jnp.einsum('nthd,ntjhd->nthj', q, kg).astype(F32) * (dh ** -0.5) + bias
    logits = jnp.where(valid[:, None, :], logits, -jnp.inf)
    o, lse = softmax_attend(logits, vg, 'nthj,ntjhd->nthd')
    new_buf = jnp.stack([kc[:, -wb:], vc[:, -wb:]], axis=2)
    return o, lse, new_buf


def merge_groups(outs, lses):
    w = jax.nn.softmax(jnp.stack(lses, 0), axis=0)
    return jnp.einsum('gbsh,gbshd->bshd', w, jnp.stack(outs, 0).astype(F32))


def merge_branches(ya, yb, gates, w_pa, w_pb, w_out):
    ba = jnp.einsum('bsi,id->bsd', ya, w_pa)
    bb = jnp.einsum('bsi,id->bsd', yb, w_pb)
    return jnp.einsum('bsi,id->bsd', gates[:, :, 0] * ba + gates[:, :, 1] * bb, w_out)


def token_mix_prompt(x, rel_bias, w_in, b_gate, conv_w, a_log, dt_bias, o_norm_w, w_pa, w_pb, w_out):
    b, s, _ = x.shape
    qkv_a, z, beta_logit, alpha_in, qkv_b, gates = in_proj(x, w_in, b_gate)
    xa = jnp.concatenate([jnp.zeros((b, CONV_W - 1, CONV_DIM), qkv_a.dtype), qkv_a], axis=1)
    new_conv = xa[:, -(CONV_W - 1):]
    q, k, v, beta, g = delta_inputs(causal_conv_silu(xa, conv_w), beta_logit, alpha_in, a_log, dt_bias)
    o, s_new = gated_delta_chunked(q, k, v, beta, g, jnp.zeros((b, H_A, DK_A, DV_A), F32))
    ya = delta_out(o, z, o_norm_w, x.dtype)
    outs, lses, bufs = [], [], []
    for gi, (win, dil) in enumerate(GROUPS):
        kg, vg = qkv_b[:, :, gi, 1], qkv_b[:, :, gi, 2]
        o_g, lse_g = dilated_attn_prompt(qkv_b[:, :, gi, 0], kg, vg, dil, win // dil,
                                         rel_bias[:, gi * H_B:(gi + 1) * H_B])
        keep = min(win, s)
        outs.append(o_g)
        lses.append(lse_g)
        bufs.append(jnp.stack([kg[:, -keep:], vg[:, -keep:]], axis=2))
    yb = merge_groups(outs, lses).reshape(b, s, H_B * DH_B).astype(x.dtype)
    return merge_branches(ya, yb, gates, w_pa, w_pb, w_out), bufs, new_conv, s_new


def token_mix_sample(x, kv_bufs, conv_buf, s_delta, rel_bias, w_in, b_gate, conv_w, a_log, dt_bias, o_norm_w, w_pa, w_pb, w_out):
    n, t, _ = x.shape
    qkv_a, z, beta_logit, alpha_in, qkv_b, gates = in_proj(x, w_in, b_gate)
    xa = jnp.concatenate([conv_buf.astype(qkv_a.dtype), qkv_a], axis=1)
    new_conv = xa[:, -(CONV_W - 1):]
    q, k, v, beta, g = delta_inputs(causal_conv_silu(xa, conv_w), beta_logit, alpha_in, a_log, dt_bias)
    o, s_new = gated_delta_recurrent(q, k, v, beta, g, s_delta.astype(F32))
    ya = delta_out(o, z, o_norm_w, x.dtype)
    outs, lses, bufs = [], [], []
    for gi, (win, dil) in enumerate(GROUPS):
        o_g, lse_g, buf = dilated_attn_sample(qkv_b[:, :, gi, 0], qkv_b[:, :, gi, 1], qkv_b[:, :, gi, 2],
                                              kv_bufs[gi], dil, win // dil,
                                              rel_bias[:, gi * H_B:(gi + 1) * H_B])
        outs.append(o_g)
        lses.append(lse_g)
        bufs.append(buf)
    yb = merge_groups(outs, lses).reshape(n, t, H_B * DH_B).astype(x.dtype)
    return merge_branches(ya, yb, gates, w_pa, w_pb, w_out), bufs, new_conv, s_new


def moe(x, router_w, router_b, w_gu, b_gu, w_dn, b_dn):
    t, d = x.shape
    logits = jnp.einsum('td,de->te', x, router_w).astype(F32) + router_b.astype(F32)
    top_val, top_idx = lax.top_k(logits, TOP_K)
    gate = jax.nn.softmax(top_val, axis=-1)
    n_assign = t * TOP_K
    blk = min(MOE_BLOCK, 1 << max(0, (-(-n_assign // N_EXPERTS) - 1).bit_length()))
    n_blocks = -(-n_assign // blk) + N_EXPERTS
    n_slots = n_blocks * blk
    flat_e = top_idx.reshape(-1).astype(jnp.int32)
    order = jnp.argsort(flat_e)
    e_sorted = flat_e[order]
    counts = jax.ops.segment_sum(jnp.ones_like(flat_e), flat_e, num_segments=N_EXPERTS)
    padded = (counts + blk - 1) // blk * blk
    pad_end = jnp.cumsum(padded)
    pad_start = pad_end - padded
    seg_start = jnp.cumsum(counts) - counts
    dest = pad_start[e_sorted] + jnp.arange(n_assign, dtype=jnp.int32) - seg_start[e_sorted]
    slot_tok = jnp.full((n_slots,), t, jnp.int32).at[dest].set((order // TOP_K).astype(jnp.int32))
    slot_gate = jnp.zeros((n_slots,), F32).at[dest].set(gate.reshape(-1)[order])
    block_e = jnp.minimum(jnp.searchsorted(pad_end, jnp.arange(n_blocks, dtype=jnp.int32) * blk, side='right'),
                          N_EXPERTS - 1)
    x_pad = jnp.concatenate([x, jnp.zeros((1, d), x.dtype)], axis=0)
    xb = x_pad[slot_tok].reshape(n_blocks, blk, d)

    def expert_block(args):
        xe, e = args
        h = xe @ w_gu[e] + b_gu[e]
        glu = jnp.minimum(h[:, :D_FF], SWIGLU_LIMIT)
        lin = jnp.clip(h[:, D_FF:], -SWIGLU_LIMIT, SWIGLU_LIMIT)
        a = glu * jax.nn.sigmoid(SWIGLU_ALPHA * glu) * (lin + 1.0)
        return a @ w_dn[e] + b_dn[e]

    yb = lax.map(expert_block, (xb, block_e)).reshape(n_slots, d)
    yb = yb * slot_gate[:, None].astype(yb.dtype)
    return jnp.zeros((t + 1, d), yb.dtype).at[slot_tok].add(yb)[:t].astype(x.dtype)


def post_block(x, mix, ln1_g, ln1_b, router_w, router_b, w_gu, b_gu, w_dn, b_dn, ln2_g, ln2_b):
    h = layer_norm(DN_ALPHA * x + mix.astype(x.dtype), ln1_g, ln1_b)
    f = moe(h.reshape(-1, D_MODEL), router_w, router_b, w_gu, b_gu, w_dn, b_dn).reshape(h.shape)
    return layer_norm(DN_ALPHA * h + f, ln2_g, ln2_b)


def setup_inputs(seed: int = 0) -> dict:
    key = jax.random.key(seed)
    ks = jax.random.split(key, 32)

    def nrm(k, shape, scale):
        return jax.random.normal(k, shape, F32) * scale

    dt = jnp.exp(jax.random.uniform(ks[10], (DEPTH, H_A), F32, math.log(1e-3), math.log(1e-1)))
    return {
        'x_prompt': nrm(ks[0], (BATCH, SEQ, D_MODEL), 1.0),
        'x_sample': nrm(ks[1], (DEC_BATCH, DEC_SEQ, D_MODEL), 1.0),
        'cache_kv_w128': nrm(ks[2], (DEPTH, DEC_BATCH, min(GROUPS[0][0], PAST_LEN), 2, H_B, DH_B), 1.0),
        'cache_kv_w512': nrm(ks[3], (DEPTH, DEC_BATCH, min(GROUPS[1][0], PAST_LEN), 2, H_B, DH_B), 1.0),
        'cache_kv_w2048': nrm(ks[4], (DEPTH, DEC_BATCH, min(GROUPS[2][0], PAST_LEN), 2, H_B, DH_B), 1.0),
        'state_conv': nrm(ks[5], (DEPTH, DEC_BATCH, CONV_W - 1, CONV_DIM), 1.0),
        'state_delta': nrm(ks[6], (DEPTH, DEC_BATCH, H_A, DK_A, DV_A), 0.3),
        'rel_bias': nrm(ks[7], (NUM_BUCKETS, H_ATTN), 0.5),
        'w_in': nrm(ks[8], (DEPTH, D_MODEL, N_IN), D_MODEL ** -0.5),
        'b_gate': nrm(ks[9], (DEPTH, 2 * D_MODEL), 0.02),
        'conv_w': nrm(ks[11], (DEPTH, CONV_W, CONV_DIM), CONV_W ** -0.5),
        'a_log': jnp.log(jax.random.uniform(ks[12], (DEPTH, H_A), F32, 1.0, 16.0)),
        'dt_bias': jnp.log(jnp.expm1(dt)),
        'o_norm_w': 1.0 + nrm(ks[13], (DEPTH, DV_A), 0.02),
        'w_branch_a': nrm(ks[14], (DEPTH, H_A * DV_A, D_MODEL), (H_A * DV_A) ** -0.5),
        'w_branch_b': nrm(ks[15], (DEPTH, H_B * DH_B, D_MODEL), (H_B * DH_B) ** -0.5),
        'w_out': nrm(ks[16], (DEPTH, D_MODEL, D_MODEL), DN_BETA * D_MODEL ** -0.5),
        'ln1_g': 1.0 + nrm(ks[17], (DEPTH, D_MODEL), 0.02),
        'ln1_b': nrm(ks[18], (DEPTH, D_MODEL), 0.02),
        'router_w': nrm(ks[19], (DEPTH, D_MODEL, N_EXPERTS), D_MODEL ** -0.5),
        'router_b': nrm(ks[20], (DEPTH, N_EXPERTS), 0.01),
        'w_gu': nrm(ks[21], (DEPTH, N_EXPERTS, D_MODEL, 2 * D_FF), D_MODEL ** -0.5),
        'b_gu': nrm(ks[22], (DEPTH, N_EXPERTS, 2 * D_FF), 0.02),
        'w_dn': nrm(ks[23], (DEPTH, N_EXPERTS, D_FF, D_MODEL), DN_BETA * D_FF ** -0.5),
        'b_dn': nrm(ks[24], (DEPTH, N_EXPERTS, D_MODEL), 0.02),
        'ln2_g': 1.0 + nrm(ks[25], (DEPTH, D_MODEL), 0.02),
        'ln2_b': nrm(ks[26], (DEPTH, D_MODEL), 0.02),
    }


def reference(x_prompt, x_sample, cache_kv_w128, cache_kv_w512, cache_kv_w2048, state_conv, state_delta, rel_bias,
              w_in, b_gate, conv_w, a_log, dt_bias, o_norm_w, w_branch_a, w_branch_b, w_out,
              ln1_g, ln1_b, router_w, router_b, w_gu, b_gu, w_dn, b_dn, ln2_g, ln2_b):
    kv_caches = (cache_kv_w128, cache_kv_w512, cache_kv_w2048)
    xp, xs = x_prompt, x_sample
    kv_p = [[] for _ in GROUPS]
    kv_s = [[] for _ in GROUPS]
    conv_p, conv_s, delta_p, delta_s = [], [], [], []
    for l in range(DEPTH):
        mix_w = (rel_bias, w_in[l], b_gate[l], conv_w[l], a_log[l], dt_bias[l], o_norm_w[l],
                 w_branch_a[l], w_branch_b[l], w_out[l])
        ffn_w = (ln1_g[l], ln1_b[l], router_w[l], router_b[l], w_gu[l], b_gu[l], w_dn[l], b_dn[l],
                 ln2_g[l], ln2_b[l])
        mp, bufs_p, cp, sp = token_mix_prompt(xp, *mix_w)
        ms, bufs_s, cs, ss = token_mix_sample(xs, (kv_caches[0][l], kv_caches[1][l], kv_caches[2][l]),
                                              state_conv[l], state_delta[l], *mix_w)
        xp = post_block(xp, mp, *ffn_w)
        xs = post_block(xs, ms, *ffn_w)
        for gi in range(N_GROUPS):
            kv_p[gi].append(bufs_p[gi])
            kv_s[gi].append(bufs_s[gi])
        conv_p.append(cp)
        conv_s.append(cs)
        delta_p.append(sp.astype(state_delta.dtype))
        delta_s.append(ss.astype(state_delta.dtype))
    return (xp, xs,
            jnp.stack(kv_p[0]), jnp.stack(kv_s[0]),
            jnp.stack(kv_p[1]), jnp.stack(kv_s[1]),
            jnp.stack(kv_p[2]), jnp.stack(kv_s[2]),
            jnp.stack(conv_p), jnp.stack(conv_s),
            jnp.stack(delta_p), jnp.stack(delta_s))
```

```python
import functools
import math

import jax
import jax.numpy as jnp
from jax import lax
from jax.experimental import pallas as pl
from jax.experimental.pallas import tpu as pltpu

F32 = jnp.float32
BF16 = jnp.bfloat16

D_MODEL = 4096
H_A, DK_A, DV_A = 16, 128, 128
CONV_W = 4
CONV_DIM = H_A * (2 * DK_A + DV_A)
CHUNK = 64
H_B, DH_B = 8, 128
GROUPS = ((128, 1), (512, 4), (2048, 16))
N_GROUPS = len(GROUPS)
Q_BLOCK = 128
NUM_BUCKETS, MAX_DISTANCE = 32, 2048
N_EXPERTS, TOP_K = 32, 4
D_FF = D_MODEL
SWIGLU_LIMIT, SWIGLU_ALPHA = 7.0, 1.702
LN_EPS, RMS_EPS = 1e-5, 1e-6

OFF_Z = CONV_DIM
OFF_BETA = OFF_Z + H_A * DV_A
OFF_ALPHA = OFF_BETA + H_A
OFF_B = OFF_ALPHA + H_A
QKV_B = 3 * H_B * DH_B
OFF_GATE = OFF_B + N_GROUPS * QKV_B

LANES = 128
VMEM_LIMIT = 56 * 1024 * 1024
ROW_TILE = 1032
MOE_BLK = 128


def _cparams(sem):
    return pltpu.CompilerParams(dimension_semantics=sem, vmem_limit_bytes=VMEM_LIMIT)


def _dot(a, b):
    return jnp.dot(a.astype(BF16), b.astype(BF16), preferred_element_type=F32)


def _dot_nt(a, b):
    return lax.dot_general(a.astype(BF16), b.astype(BF16), (((1,), (1,)), ((), ())),
                           preferred_element_type=F32)


def _dot_tn(a, b):
    return lax.dot_general(a.astype(BF16), b.astype(BF16), (((0,), (0,)), ((), ())),
                           preferred_element_type=F32)


def _dot_hi(a, b):
    return jnp.dot(a, b, precision=lax.Precision.HIGHEST, preferred_element_type=F32)


def _silu(x):
    return x * jax.nn.sigmoid(x)


def _mm_body(*refs, sigmoid_bias):
    if sigmoid_bias:
        x_ref, w_ref, b_ref, o_ref = refs
    else:
        x_ref, w_ref, o_ref = refs
    acc = jnp.dot(x_ref[...], w_ref[...].astype(BF16), preferred_element_type=F32)
    if sigmoid_bias:
        acc = jax.nn.sigmoid(acc + b_ref[...])
    o_ref[...] = acc.astype(o_ref.dtype)


def _mm(x, w, *, n_cols, col0=0, tm, tn, bias=None, out_dtype=F32, name):
    m, k = x.shape
    c0 = col0 // tn
    in_specs = [pl.BlockSpec((tm, k), lambda i, j: (i, 0)),
                pl.BlockSpec((k, tn), lambda i, j: (0, j + c0))]
    args = [x, w]
    if bias is not None:
        in_specs.append(pl.BlockSpec((1, tn), lambda i, j: (0, j)))
        args.append(bias)
    return pl.pallas_call(
        functools.partial(_mm_body, sigmoid_bias=bias is not None),
        grid=(m // tm, n_cols // tn),
        in_specs=in_specs,
        out_specs=pl.BlockSpec((tm, tn), lambda i, j: (i, j)),
        out_shape=jax.ShapeDtypeStruct((m, n_cols), out_dtype),
        compiler_params=_cparams(("parallel", "parallel")),
        name=name,
    )(*args)


def _bg_body(p_ref, a_ref, dt_ref, beta_ref, g_ref, gc_ref):
    p = p_ref[...]
    tt = p.shape[0]
    beta_ref[...] = jax.nn.sigmoid(p)
    g = -jnp.exp(a_ref[...]) * jax.nn.softplus(p + dt_ref[...])
    g_ref[...] = g
    ri = lax.broadcasted_iota(jnp.int32, (tt, tt), 0)
    ci = lax.broadcasted_iota(jnp.int32, (tt, tt), 1)
    same_chunk_before = ((ri // CHUNK) == (ci // CHUNK)) & (ci <= ri)
    gc_ref[...] = _dot_hi(same_chunk_before.astype(F32), g)


def _beta_g(p_ba, a_log_l, dt_bias_l):
    rows = p_ba.shape[0]
    tt = 3 * CHUNK
    zeros = jnp.zeros((LANES - 2 * H_A,), F32)
    a_vec = jnp.concatenate([jnp.zeros((H_A,), F32), a_log_l, zeros]).reshape(1, LANES)
    dt_vec = jnp.concatenate([jnp.zeros((H_A,), F32), dt_bias_l, zeros]).reshape(1, LANES)
    row_spec = pl.BlockSpec((tt, LANES), lambda i: (i, 0))
    vec_spec = pl.BlockSpec((1, LANES), lambda i: (0, 0))
    shp = jax.ShapeDtypeStruct((rows, LANES), F32)
    beta, g, gc = pl.pallas_call(
        _bg_body, grid=(rows // tt,),
        in_specs=[row_spec, vec_spec, vec_spec],
        out_specs=[row_spec, row_spec, row_spec],
        out_shape=[shp, shp, shp],
        compiler_params=_cparams(("parallel",)),
        name="deltanet_gates",
    )(p_ba, a_vec, dt_vec)
    return beta[:, :H_A], g[:, H_A:2 * H_A], gc[:, H_A:2 * H_A]


def _conv_body(cur_ref, halo_ref, w_ref, o_ref, *, seq, tt):
    i = pl.program_id(0)
    j = pl.program_id(1)
    cur = cur_ref[...]
    halo = jnp.where((i * tt) % seq == 0, 0.0, halo_ref[...])
    xh = jnp.concatenate([halo, cur], axis=0)
    w = w_ref[...]
    acc = xh[5:5 + tt] * w[0:1]
    for t in range(1, CONV_W):
        acc = acc + xh[5 + t:5 + t + tt] * w[t:t + 1]
    y = _silu(acc)
    ct = y.shape[1]
    n_qk_blocks = 2 * H_A * DK_A // ct
    is_q = j < H_A * DK_A // ct
    is_qk = j < n_qk_blocks
    scale = jnp.where(is_q, DK_A ** -0.5, 1.0)
    for h in range(ct // DK_A):
        sl = slice(h * DK_A, (h + 1) * DK_A)
        yh = y[:, sl]
        nrm = yh * lax.rsqrt(jnp.sum(yh * yh, axis=-1, keepdims=True) + RMS_EPS) * scale
        o_ref[:, sl] = jnp.where(is_qk, nrm, yh)


def _conv_qkv_prompt(p_a, conv_w_l, n_rows, seq):
    tt, ct = 256, 1024
    return pl.pallas_call(
        functools.partial(_conv_body, seq=seq, tt=tt),
        grid=(n_rows // tt, CONV_DIM // ct),
        in_specs=[pl.BlockSpec((tt, ct), lambda i, j: (i, j)),
                  pl.BlockSpec((8, ct), lambda i, j: (jnp.maximum(i * (tt // 8) - 1, 0), j)),
                  pl.BlockSpec((CONV_W, ct), lambda i, j: (0, j))],
        out_specs=pl.BlockSpec((tt, ct), lambda i, j: (i, j)),
        out_shape=jax.ShapeDtypeStruct((n_rows, CONV_DIM), F32),
        compiler_params=_cparams(("parallel", "parallel")),
        name="conv_silu_l2norm",
    )(p_a, p_a, conv_w_l)


def _gdn_body(q_ref, k_ref, v_ref, z_ref, b_ref, gc_ref, gr_ref, nw_ref, ya_ref, s_ref, *, hb):
    c = pl.program_id(2)

    @pl.when(c == 0)
    def _():
        s_ref[...] = jnp.zeros_like(s_ref)

    C = CHUNK
    ri = lax.broadcasted_iota(jnp.int32, (C, C), 0)
    ci = lax.broadcasted_iota(jnp.int32, (C, C), 1)
    tril = ci <= ri
    strict = ci < ri
    eye = (ri == ci).astype(F32)
    nw = nw_ref[...]
    for hh in range(hb):
        sl = slice(hh * DK_A, (hh + 1) * DK_A)
        q = q_ref[:, sl]
        k = k_ref[:, sl]
        v = v_ref[:, sl]
        b = b_ref[0, 0, 0][:, hh:hh + 1]
        gc = gc_ref[0, 0, 0][:, hh:hh + 1]
        gr = gr_ref[0, 0, 0][hh:hh + 1, :]
        decay = jnp.where(tril, jnp.exp(jnp.where(tril, gc - gr, 0.0)), 0.0)
        kb = k * b
        m = jnp.where(strict, _dot_nt(kb, k) * decay, 0.0)
        eg = jnp.exp(gc)
        rhs = jnp.concatenate([v * b, kb * eg], axis=-1)
        y = -m
        inv = eye + y
        for _ in range(5):
            y = _dot_hi(y, y)
            inv = inv + _dot_hi(inv, y)
        sol = _dot_hi(inv, rhs)
        u = sol[:, :DV_A]
        w = sol[:, DV_A:]
        attn = jnp.where(tril, _dot_nt(q, k) * decay, 0.0)
        g_last = gc[C - 1:C, :]
        kd = k * jnp.exp(g_last - gc)
        s = s_ref[0, hh]
        v_new = u - _dot(w, s)
        o = _dot(q * eg, s) + _dot(attn, v_new)
        s_ref[0, hh] = s * jnp.exp(g_last) + _dot_tn(kd, v_new)
        o = o * lax.rsqrt(jnp.mean(o * o, axis=-1, keepdims=True) + RMS_EPS) * nw
        ya_ref[:, sl] = (o * _silu(z_ref[:, sl])).astype(ya_ref.dtype)


def _gdn_prompt(qkv, p_a, beta, gc, o_norm_w_l, batch, seq):
    hb = 4
    nhb = H_A // hb
    nc = seq // CHUNK
    w = hb * DK_A

    def per_head_cols(a):
        return a.reshape(batch, nc, CHUNK, nhb, hb).transpose(0, 3, 1, 2, 4)

    beta_c = per_head_cols(beta)
    gc_c = per_head_cols(gc)
    gc_r = gc_c.transpose(0, 1, 2, 4, 3)
    col_spec = pl.BlockSpec((1, 1, 1, CHUNK, hb), lambda b, h, c: (b, h, c, 0, 0))
    row_spec = pl.BlockSpec((1, 1, 1, hb, CHUNK), lambda b, h, c: (b, h, c, 0, 0))

    def tok_spec(col_block0):
        return pl.BlockSpec((CHUNK, w), lambda b, h, c: (b * nc + c, col_block0 + h))

    ya, s_fin = pl.pallas_call(
        functools.partial(_gdn_body, hb=hb),
        grid=(batch, nhb, nc),
        in_specs=[tok_spec(0), tok_spec(H_A * DK_A // w), tok_spec(2 * H_A * DK_A // w),
                  tok_spec(OFF_Z // w), col_spec, col_spec, row_spec,
                  pl.BlockSpec((1, DV_A), lambda b, h, c: (0, 0))],
        out_specs=[tok_spec(0),
                   pl.BlockSpec((1, hb, DK_A, DV_A), lambda b, h, c: (b, h, 0, 0))],
        out_shape=[jax.ShapeDtypeStruct((batch * seq, H_A * DV_A), BF16),
                   jax.ShapeDtypeStruct((batch, H_A, DK_A, DV_A), F32)],
        compiler_params=_cparams(("parallel", "parallel", "arbitrary")),
        name="gated_delta_chunked",
    )(qkv, qkv, qkv, p_a, beta_c, gc_c, gc_r, o_norm_w_l.reshape(1, DV_A))
    return ya, s_fin


def _gdn_sample_body(pa_ref, cbuf_ref, cw_ref, bg_ref, s_ref, nw_ref, ya_ref, conv_ref, so_ref):
    new = pa_ref[0, :, :CONV_DIM]
    buf = cbuf_ref[0]
    w = cw_ref[...]
    acc = new * w[3:4]
    for t in range(CONV_W - 1):
        acc = acc + buf[t:t + 1] * w[t:t + 1]
    y = _silu(acc)
    conv_ref[0, 0:2, :] = buf[1:3]
    conv_ref[0, 2:3, :] = new
    z = pa_ref[0, :, OFF_Z:OFF_BETA]
    bg = bg_ref[0]
    nw = nw_ref[...]
    ri = lax.broadcasted_iota(jnp.int32, (DK_A, DK_A), 0)
    ci = lax.broadcasted_iota(jnp.int32, (DK_A, DK_A), 1)
    eye = ri == ci

    def as_col(row):
        return jnp.sum(jnp.where(eye, row, 0.0), axis=1, keepdims=True)

    for h in range(H_A):
        qh = y[:, h * DK_A:(h + 1) * DK_A]
        kh = y[:, (H_A + h) * DK_A:(H_A + h + 1) * DK_A]
        vh = y[:, (2 * H_A + h) * DK_A:(2 * H_A + h + 1) * DK_A]
        qh = qh * lax.rsqrt(jnp.sum(qh * qh, axis=-1, keepdims=True) + RMS_EPS) * (DK_A ** -0.5)
        kh = kh * lax.rsqrt(jnp.sum(kh * kh, axis=-1, keepdims=True) + RMS_EPS)
        beta = bg[0:1, h:h + 1]
        g = bg[1:2, h:h + 1]
        k_col = as_col(kh)
        s = s_ref[0, h] * jnp.exp(g)
        kv = jnp.sum(s * k_col, axis=0, keepdims=True)
        s = s + k_col * ((vh - kv) * beta)
        so_ref[0, h] = s
        o = jnp.sum(s * as_col(qh), axis=0, keepdims=True)
        o = o * lax.rsqrt(jnp.mean(o * o, axis=-1, keepdims=True) + RMS_EPS) * nw
        ya_ref[0, :, h * DV_A:(h + 1) * DV_A] = (o * _silu(z[:, h * DV_A:(h + 1) * DV_A])).astype(ya_ref.dtype)


def _gdn_sample(p_a_s, conv_buf, conv_w_l, beta_s, g_s, s_delta, o_norm_w_l):
    n = p_a_s.shape[0]
    bg = jnp.stack([beta_s, g_s], axis=1)
    return pl.pallas_call(
        _gdn_sample_body, grid=(n,),
        in_specs=[pl.BlockSpec((1, 1, OFF_BETA), lambda i: (i, 0, 0)),
                  pl.BlockSpec((1, CONV_W - 1, CONV_DIM), lambda i: (i, 0, 0)),
                  pl.BlockSpec((CONV_W, CONV_DIM), lambda i: (0, 0)),
                  pl.BlockSpec((1, 2, H_A), lambda i: (i, 0, 0)),
                  pl.BlockSpec((1, H_A, DK_A, DV_A), lambda i: (i, 0, 0, 0)),
                  pl.BlockSpec((1, DV_A), lambda i: (0, 0))],
        out_specs=[pl.BlockSpec((1, 1, H_A * DV_A), lambda i: (i, 0, 0)),
                   pl.BlockSpec((1, CONV_W - 1, CONV_DIM), lambda i: (i, 0, 0)),
                   pl.BlockSpec((1, H_A, DK_A, DV_A), lambda i: (i, 0, 0, 0))],
        out_shape=[jax.ShapeDtypeStruct((n, 1, H_A * DV_A), BF16),
                   jax.ShapeDtypeStruct((n, CONV_W - 1, CONV_DIM), F32),
                   jax.ShapeDtypeStruct((n, H_A, DK_A, DV_A), F32)],
        compiler_params=_cparams(("parallel",)),
        name="gated_delta_step",
    )(p_a_s, conv_buf, conv_w_l, bg, s_delta, o_norm_w_l.reshape(1, DV_A))


def _t5_bucket(dist):
    max_exact = NUM_BUCKETS // 2
    d = jnp.maximum(dist, 1).astype(F32)
    large = max_exact + (jnp.log(d / max_exact) / math.log(MAX_DISTANCE / max_exact)
                         * (NUM_BUCKETS - max_exact)).astype(jnp.int32)
    large = jnp.minimum(large, NUM_BUCKETS - 1)
    return jnp.where(dist < max_exact, dist, large)


def _attn_prompt_body(q_ref, kp_ref, kc_ref, vp_ref, vc_ref, bias_ref, o_ref, lse_ref):
    n = pl.program_id(2)
    kj = lax.broadcasted_iota(jnp.int32, (Q_BLOCK, 2 * Q_BLOCK), 1)
    in_seq = (n > 0) | (kj >= Q_BLOCK)
    for h in range(H_B):
        sl = slice(h * DH_B, (h + 1) * DH_B)
        kk = jnp.concatenate([kp_ref[:, sl], kc_ref[:, sl]], axis=0)
        vv = jnp.concatenate([vp_ref[:, sl], vc_ref[:, sl]], axis=0)
        logits = _dot_nt(q_ref[:, sl], kk) * (DH_B ** -0.5) + bias_ref[h]
        logits = jnp.where(in_seq, logits, -jnp.inf)
        mx = jnp.max(logits, axis=-1, keepdims=True)
        e = jnp.exp(logits - mx)
        den = jnp.sum(e, axis=-1, keepdims=True)
        o_ref[:, sl] = _dot(e / den, vv)
        lse_ref[:, sl] = jnp.broadcast_to(mx + jnp.log(den), (Q_BLOCK, DH_B))


def _attn_prompt(p_b, gi, bias_tab, batch, seq):
    _, dil = GROUPS[gi]
    rows = p_b.shape[0]
    length = seq // dil
    nb = length // Q_BLOCK
    hw = H_B * DH_B
    view = p_b.reshape(rows // dil, dil * N_GROUPS * QKV_B)
    cpb = N_GROUPS * QKV_B // hw

    def spec(which, prev):
        def imap(b, r, n):
            nn = jnp.maximum(n - 1, 0) if prev else n
            return (b * nb + nn, r * cpb + gi * 3 + which)
        return pl.BlockSpec((Q_BLOCK, hw), imap)

    out_spec = pl.BlockSpec((Q_BLOCK, hw), lambda b, r, n: (b * nb + n, r))
    shp = jax.ShapeDtypeStruct((batch * seq // dil, dil * hw), F32)
    o, lse = pl.pallas_call(
        _attn_prompt_body, grid=(batch, dil, nb),
        in_specs=[spec(0, False), spec(1, True), spec(1, False), spec(2, True), spec(2, False),
                  pl.BlockSpec((H_B, Q_BLOCK, 2 * Q_BLOCK), lambda b, r, n: (0, 0, 0))],
        out_specs=[out_spec, out_spec],
        out_shape=[shp, shp],
        compiler_params=_cparams(("parallel", "parallel", "parallel")),
        name=f"dilated_attn_prompt_g{gi}",
    )(view, view, view, view, view, bias_tab)
    return o.reshape(batch * seq, hw), lse.reshape(batch * seq, hw)


def _merge_body(o0, o1, o2, l0, l1, l2, yb_ref):
    ls = [l0[...], l1[...], l2[...]]
    mx = jnp.maximum(jnp.maximum(ls[0], ls[1]), ls[2])
    es = [jnp.exp(l - mx) for l in ls]
    den = es[0] + es[1] + es[2]
    acc = (es[0] / den) * o0[...] + (es[1] / den) * o1[...] + (es[2] / den) * o2[...]
    yb_ref[...] = acc.astype(yb_ref.dtype)


def _merge_groups(outs, lses):
    rows, hw = outs[0].shape
    tt = 512
    spec = pl.BlockSpec((tt, hw), lambda i: (i, 0))
    return pl.pallas_call(
        _merge_body, grid=(rows // tt,),
        in_specs=[spec] * 6, out_specs=spec,
        out_shape=jax.ShapeDtypeStruct((rows, hw), BF16),
        compiler_params=_cparams(("parallel",)),
        name="merge_attention_groups",
    )(*outs, *lses)


def _attn_sample_body(pb_ref, c0_ref, c1_ref, c2_ref, bias_ref, yb_ref):
    hw = H_B * DH_B
    caches = (c0_ref, c1_ref, c2_ref)
    for h in range(H_B):
        outs, lses = [], []
        for gi in range(N_GROUPS):
            base = gi * QKV_B + h * DH_B
            q = pb_ref[0, :, base:base + DH_B]
            k_new = pb_ref[0, :, base + hw:base + hw + DH_B]
            v_new = pb_ref[0, :, base + 2 * hw:base + 2 * hw + DH_B]
            kc = caches[gi][0, :, h * DH_B:(h + 1) * DH_B]
            vc = caches[gi][0, :, hw + h * DH_B:hw + (h + 1) * DH_B]
            bias = bias_ref[gi, h]
            q8 = jnp.broadcast_to(q, (8, DH_B))
            lc = _dot_nt(q8, kc)[0:1] * (DH_B ** -0.5) + bias[0:1]
            qb = q.astype(BF16).astype(F32)
            ln = (jnp.sum(qb * k_new.astype(BF16).astype(F32), axis=-1, keepdims=True) * (DH_B ** -0.5)
                  + bias[1:2, 0:1])
            mx = jnp.maximum(jnp.max(lc, axis=-1, keepdims=True), ln)
            ec = jnp.exp(lc - mx)
            en = jnp.exp(ln - mx)
            den = jnp.sum(ec, axis=-1, keepdims=True) + en
            pc = jnp.broadcast_to(ec / den, (8, Q_BLOCK))
            o = _dot(pc, vc)[0:1] + (en / den).astype(BF16).astype(F32) * v_new.astype(BF16).astype(F32)
            outs.append(o)
            lses.append(mx + jnp.log(den))
        mx = jnp.maximum(jnp.maximum(lses[0], lses[1]), lses[2])
        es = [jnp.exp(l - mx) for l in lses]
        den = es[0] + es[1] + es[2]
        acc = (es[0] / den) * outs[0] + (es[1] / den) * outs[1] + (es[2] / den) * outs[2]
        yb_ref[0, :, h * DH_B:(h + 1) * DH_B] = acc.astype(yb_ref.dtype)


def _attn_sample(p_b_s, caches, bias_s):
    n = p_b_s.shape[0]
    hw = H_B * DH_B
    views, specs = [], []
    for (win, dil), c in zip(GROUPS, caches):
        wb = c.shape[1]
        views.append(c.reshape(n, wb // dil, dil * 2 * hw))
        specs.append(pl.BlockSpec((1, wb // dil, 2 * hw), lambda i: (i, 0, 0)))
    return pl.pallas_call(
        _attn_sample_body, grid=(n,),
        in_specs=[pl.BlockSpec((1, 1, N_GROUPS * QKV_B), lambda i: (i, 0, 0))] + specs
                 + [pl.BlockSpec((N_GROUPS, H_B, 2, Q_BLOCK), lambda i: (0, 0, 0, 0))],
        out_specs=pl.BlockSpec((1, 1, hw), lambda i: (i, 0, 0)),
        out_shape=jax.ShapeDtypeStruct((n, 1, hw), BF16),
        compiler_params=_cparams(("parallel",)),
        name="dilated_attn_sample",
    )(p_b_s, *views, bias_s)


def _branch_body(ya_ref, yb_ref, wa_ref, wb_ref, ga_ref, gb_ref, o_ref):
    ba = jnp.dot(ya_ref[...], wa_ref[...].astype(BF16), preferred_element_type=F32)
    bb = jnp.dot(yb_ref[...], wb_ref[...].astype(BF16), preferred_element_type=F32)
    o_ref[...] = (ga_ref[...] * ba + gb_ref[...] * bb).astype(o_ref.dtype)


def _branch_merge(ya, yb, w_pa, w_pb, gates):
    rows = ya.shape[0]
    tm, tn = ROW_TILE, 512
    nj = D_MODEL // tn
    return pl.pallas_call(
        _branch_body, grid=(rows // tm, nj),
        in_specs=[pl.BlockSpec((tm, ya.shape[1]), lambda i, j: (i, 0)),
                  pl.BlockSpec((tm, yb.shape[1]), lambda i, j: (i, 0)),
                  pl.BlockSpec((w_pa.shape[0], tn), lambda i, j: (0, j)),
                  pl.BlockSpec((w_pb.shape[0], tn), lambda i, j: (0, j)),
                  pl.BlockSpec((tm, tn), lambda i, j: (i, j)),
                  pl.BlockSpec((tm, tn), lambda i, j: (i, nj + j))],
        out_specs=pl.BlockSpec((tm, tn), lambda i, j: (i, j)),
        out_shape=jax.ShapeDtypeStruct((rows, D_MODEL), BF16),
        compiler_params=_cparams(("parallel", "parallel")),
        name="branch_proj_gate",
    )(ya, yb, w_pa, w_pb, gates, gates)


def _ln_body(x_ref, r_ref, g_ref, b_ref, *out_refs, alpha):
    v = alpha * x_ref[...] + r_ref[...]
    mu = jnp.mean(v, axis=-1, keepdims=True)
    var = jnp.mean(jnp.square(v - mu), axis=-1, keepdims=True)
    y = (v - mu) * lax.rsqrt(var + LN_EPS) * g_ref[...] + b_ref[...]
    for o_ref in out_refs:
        o_ref[...] = y.astype(o_ref.dtype)


def _residual_ln(x, r, g, b, alpha, out_dtypes, name):
    rows, d = x.shape
    tt = 344
    spec = pl.BlockSpec((tt, d), lambda i: (i, 0))
    vec = pl.BlockSpec((1, d), lambda i: (0, 0))
    return pl.pallas_call(
        functools.partial(_ln_body, alpha=alpha), grid=(rows // tt,),
        in_specs=[spec, spec, vec, vec],
        out_specs=[spec] * len(out_dtypes),
        out_shape=[jax.ShapeDtypeStruct((rows, d), dt) for dt in out_dtypes],
        compiler_params=_cparams(("parallel",)),
        name=name,
    )(x, r, g.reshape(1, d), b.reshape(1, d))


def _router_body(h_ref, w_ref, b_ref, idx_ref, gate_ref):
    logits = _dot_hi(h_ref[...], w_ref[...]) + b_ref[...]
    lane = lax.broadcasted_iota(jnp.int32, logits.shape, 1)
    vals, idxs = [], []
    for _ in range(TOP_K):
        mx = jnp.max(logits, axis=-1, keepdims=True)
        idx = jnp.min(jnp.where(logits == mx, lane, LANES), axis=-1, keepdims=True)
        vals.append(mx)
        idxs.append(idx)
        logits = jnp.where(lane == idx, -jnp.inf, logits)
    es = [jnp.exp(v - vals[0]) for v in vals]
    den = es[0] + es[1] + es[2] + es[3]
    idx_out = jnp.zeros(logits.shape, jnp.int32)
    gate_out = jnp.zeros(logits.shape, F32)
    for k in range(TOP_K):
        idx_out = jnp.where(lane == k, idxs[k], idx_out)
        gate_out = jnp.where(lane == k, es[k] / den, gate_out)
    idx_ref[...] = idx_out
    gate_ref[...] = gate_out


def _router(h, router_w_l, router_b_l):
    rows, d = h.shape
    tt = 344
    w = jnp.pad(router_w_l, ((0, 0), (0, LANES - N_EXPERTS)))
    b = jnp.concatenate([router_b_l, jnp.full((LANES - N_EXPERTS,), -jnp.inf, F32)]).reshape(1, LANES)
    spec = pl.BlockSpec((tt, LANES), lambda i: (i, 0))
    return pl.pallas_call(
        _router_body, grid=(rows // tt,),
        in_specs=[pl.BlockSpec((tt, d), lambda i: (i, 0)),
                  pl.BlockSpec((d, LANES), lambda i: (0, 0)),
                  pl.BlockSpec((1, LANES), lambda i: (0, 0))],
        out_specs=[spec, spec],
        out_shape=[jax.ShapeDtypeStruct((rows, LANES), jnp.int32),
                   jax.ShapeDtypeStruct((rows, LANES), F32)],
        compiler_params=_cparams(("parallel",)),
        name="router_top4",
    )(h, w, b)


def _gmm1_body(be_ref, first_ref, nused_ref, x_ref, wg_ref, wl_ref, bg_ref, bl_ref, a_ref, wg_bf, wl_bf):
    i = pl.program_id(1)

    @pl.when(first_ref[i] == 1)
    def _():
        wg_bf[...] = wg_ref[0].astype(BF16)
        wl_bf[...] = wl_ref[0].astype(BF16)

    @pl.when(i < nused_ref[0])
    def _():
        x = x_ref[...]
        hg = jnp.dot(x, wg_bf[...], preferred_element_type=F32) + bg_ref[0]
        hl = jnp.dot(x, wl_bf[...], preferred_element_type=F32) + bl_ref[0]
        glu = jnp.minimum(hg, SWIGLU_LIMIT)
        lin = jnp.clip(hl, -SWIGLU_LIMIT, SWIGLU_LIMIT)
        a_ref[...] = (glu * jax.nn.sigmoid(SWIGLU_ALPHA * glu) * (lin + 1.0)).astype(a_ref.dtype)


def _gmm1(xs, w_gu_l, b_gu_l, block_e, first, n_used):
    n_slots, d = xs.shape
    nb = n_slots // MOE_BLK
    tn = 512
    nj = D_FF // tn
    b3 = b_gu_l.reshape(N_EXPERTS, 1, 2 * D_FF)
    grid_spec = pltpu.PrefetchScalarGridSpec(
        num_scalar_prefetch=3, grid=(nj, nb),
        in_specs=[pl.BlockSpec((MOE_BLK, d), lambda j, i, be, fs, nu: (i, 0)),
                  pl.BlockSpec((1, d, tn), lambda j, i, be, fs, nu: (be[i], 0, j)),
                  pl.BlockSpec((1, d, tn), lambda j, i, be, fs, nu: (be[i], 0, nj + j)),
                  pl.BlockSpec((1, 1, tn), lambda j, i, be, fs, nu: (be[i], 0, j)),
                  pl.BlockSpec((1, 1, tn), lambda j, i, be, fs, nu: (be[i], 0, nj + j))],
        out_specs=pl.BlockSpec((MOE_BLK, tn), lambda j, i, be, fs, nu: (i, j)),
        scratch_shapes=[pltpu.VMEM((d, tn), BF16), pltpu.VMEM((d, tn), BF16)])
    return pl.pallas_call(
        _gmm1_body, grid_spec=grid_spec,
        out_shape=jax.ShapeDtypeStruct((n_slots, D_FF), BF16),
        compiler_params=_cparams(("arbitrary", "arbitrary")),
        name="moe_up_swiglu",
    )(block_e, first, n_used, xs, w_gu_l, w_gu_l, b3, b3)


def _gmm2_body(be_ref, first_ref, nused_ref, a_ref, w_ref, b_ref, y_ref, w_bf):
    i = pl.program_id(1)

    @pl.when(first_ref[i] == 1)
    def _():
        w_bf[...] = w_ref[0].astype(BF16)

    @pl.when(i < nused_ref[0])
    def _():
        y_ref[...] = jnp.dot(a_ref[...], w_bf[...], preferred_element_type=F32) + b_ref[0]


def _gmm2(a, w_dn_l, b_dn_l, block_e, first, n_used):
    n_slots, f = a.shape
    nb = n_slots // MOE_BLK
    tn = min(1024, D_MODEL)
    nj = D_MODEL // tn
    b3 = b_dn_l.reshape(N_EXPERTS, 1, D_MODEL)
    grid_spec = pltpu.PrefetchScalarGridSpec(
        num_scalar_prefetch=3, grid=(nj, nb),
        in_specs=[pl.BlockSpec((MOE_BLK, f), lambda j, i, be, fs, nu: (i, 0)),
                  pl.BlockSpec((1, f, tn), lambda j, i, be, fs, nu: (be[i], 0, j)),
                  pl.BlockSpec((1, 1, tn), lambda j, i, be, fs, nu: (be[i], 0, j))],
        out_specs=pl.BlockSpec((MOE_BLK, tn), lambda j, i, be, fs, nu: (i, j)),
        scratch_shapes=[pltpu.VMEM((f, tn), BF16)])
    return pl.pallas_call(
        _gmm2_body, grid_spec=grid_spec,
        out_shape=jax.ShapeDtypeStruct((n_slots, D_MODEL), F32),
        compiler_params=_cparams(("arbitrary", "arbitrary")),
        name="moe_down",
    )(block_e, first, n_used, a, w_dn_l, b3)


def _moe(h32, hbf, n_tok, router_w_l, router_b_l, w_gu_l, b_gu_l, w_dn_l, b_dn_l):
    rows = h32.shape[0]
    idx_pad, gate_pad = _router(h32, router_w_l, router_b_l)
    top_idx = idx_pad[:n_tok, :TOP_K]
    gate = gate_pad[:n_tok, :TOP_K]
    n_assign = n_tok * TOP_K
    n_blocks = -(-n_assign // MOE_BLK) + N_EXPERTS
    n_slots = n_blocks * MOE_BLK
    flat_e = top_idx.reshape(-1)
    onehot = (flat_e[:, None] == jnp.arange(N_EXPERTS, dtype=jnp.int32)[None, :]).astype(jnp.int32)
    rank = jnp.sum((jnp.cumsum(onehot, axis=0) - onehot) * onehot, axis=1)
    counts = jnp.sum(onehot, axis=0)
    padded = (counts + MOE_BLK - 1) // MOE_BLK * MOE_BLK
    pad_end = jnp.cumsum(padded)
    pad_start = pad_end - padded
    dest = pad_start[flat_e] + rank
    slot_tok = jnp.zeros((n_slots,), jnp.int32).at[dest].set(jnp.arange(n_assign, dtype=jnp.int32) // TOP_K)
    n_used = (pad_end[-1] // MOE_BLK).astype(jnp.int32)
    blk = jnp.minimum(jnp.arange(n_blocks, dtype=jnp.int32), n_used - 1)
    block_e = jnp.minimum(jnp.searchsorted(pad_end, blk * MOE_BLK, side='right'), N_EXPERTS - 1).astype(jnp.int32)
    first = jnp.concatenate([jnp.ones((1,), jnp.int32), (block_e[1:] != block_e[:-1]).astype(jnp.int32)])
    xs = hbf[slot_tok]
    a = _gmm1(xs, w_gu_l, b_gu_l, block_e, first, n_used.reshape(1))
    y = _gmm2(a, w_dn_l, b_dn_l, block_e, first, n_used.reshape(1))
    f = jnp.sum(y[dest.reshape(n_tok, TOP_K)] * gate[..., None], axis=1)
    return jnp.pad(f, ((0, rows - n_tok), (0, 0)))


def _attention_bias_tables(rel_bias):
    qi = jnp.arange(Q_BLOCK)[:, None]
    kj = jnp.arange(2 * Q_BLOCK)[None, :]
    rel = qi + Q_BLOCK - kj
    prompt_tabs, sample_tabs = [], []
    for gi, (win, dil) in enumerate(GROUPS):
        steps = win // dil
        bias_g = rel_bias[:, gi * H_B:(gi + 1) * H_B]
        tab = bias_g[_t5_bucket(jnp.maximum(rel, 0) * dil)].astype(F32).transpose(2, 0, 1)
        prompt_tabs.append(jnp.where(((rel >= 0) & (rel <= steps))[None], tab, -jnp.inf))
        bj = bias_g[_t5_bucket(jnp.arange(steps + 1) * dil)].astype(F32).T
        cache_rows = bj[:, :0:-1]
        new_tok = jnp.pad(bj[:, 0:1], ((0, 0), (0, Q_BLOCK - 1)))
        sample_tabs.append(jnp.stack([cache_rows, new_tok], axis=1))
    return prompt_tabs, jnp.stack(sample_tabs)


def kernel(x_prompt, x_sample, cache_kv_w128, cache_kv_w512, cache_kv_w2048, state_conv, state_delta, rel_bias,
           w_in, b_gate, conv_w, a_log, dt_bias, o_norm_w, w_branch_a, w_branch_b, w_out,
           ln1_g, ln1_b, router_w, router_b, w_gu, b_gu, w_dn, b_dn, ln2_g, ln2_b):
    batch, seq, d = x_prompt.shape
    n_s = x_sample.shape[0]
    n_p = batch * seq
    n_tok = n_p + n_s
    rows = 8 * ROW_TILE
    depth = w_in.shape[0]
    alpha = (2.0 * depth) ** 0.25
    kv_caches = (cache_kv_w128, cache_kv_w512, cache_kv_w2048)
    hw = H_B * DH_B

    x = jnp.concatenate([x_prompt.reshape(n_p, d), x_sample.reshape(n_s, d),
                         jnp.zeros((rows - n_tok, d), F32)], axis=0)
    prompt_tabs, sample_tab = _attention_bias_tables(rel_bias)

    kv_p = [[] for _ in GROUPS]
    kv_s = [[] for _ in GROUPS]
    conv_p, conv_s, delta_p, delta_s = [], [], [], []
    for l in range(depth):
        xb = x.astype(BF16)
        w_in_l = w_in[l]
        p_a = _mm(xb, w_in_l, n_cols=OFF_BETA, tm=ROW_TILE, tn=512, name="in_proj_deltanet")
        w_ba = jnp.pad(w_in_l[:, OFF_BETA:OFF_B], ((0, 0), (0, LANES - 2 * H_A)))
        p_ba = _mm(xb, w_ba, n_cols=LANES, tm=ROW_TILE, tn=LANES, name="in_proj_beta_alpha")
        w_bg = w_in_l[:, OFF_B:]
        p_b = _mm(xb, w_bg, n_cols=N_GROUPS * QKV_B, tm=ROW_TILE, tn=512, name="in_proj_attention")
        gates = _mm(xb, w_bg, n_cols=2 * D_MODEL, col0=N_GROUPS * QKV_B, tm=ROW_TILE, tn=512,
                    bias=b_gate[l].reshape(1, 2 * D_MODEL), name="in_proj_gates")

        beta, g, gc = _beta_g(p_ba, a_log[l], dt_bias[l])
        qkv = _conv_qkv_prompt(p_a, conv_w[l], n_p, seq)
        ya_p, s_p = _gdn_prompt(qkv, p_a, beta[:n_p], gc[:n_p], o_norm_w[l], batch, seq)
        ya_s, cs, s_s = _gdn_sample(p_a[n_p:n_tok].reshape(n_s, 1, OFF_BETA), state_conv[l], conv_w[l],
                                    beta[n_p:n_tok], g[n_p:n_tok], state_delta[l], o_norm_w[l])
        ya = jnp.concatenate([ya_p, ya_s.reshape(n_s, H_A * DV_A),
                              jnp.zeros((rows - n_tok, H_A * DV_A), BF16)], axis=0)

        outs, lses = [], []
        for gi in range(N_GROUPS):
            o_g, lse_g = _attn_prompt(p_b, gi, prompt_tabs[gi], batch, seq)
            outs.append(o_g)
            lses.append(lse_g)
        yb_p = _merge_groups(outs, lses)
        p_b_s = p_b[n_p:n_tok]
        yb_s = _attn_sample(p_b_s.reshape(n_s, 1, N_GROUPS * QKV_B), [c[l] for c in kv_caches], sample_tab)
        yb = jnp.concatenate([yb_p, yb_s.reshape(n_s, hw), jnp.zeros((rows - n_tok, hw), BF16)], axis=0)

        merged = _branch_merge(ya, yb, w_branch_a[l], w_branch_b[l], gates)
        mix = _mm(merged, w_out[l], n_cols=D_MODEL, tm=ROW_TILE, tn=512, name="out_proj")
        h32, hbf = _residual_ln(x, mix, ln1_g[l], ln1_b[l], alpha, (F32, BF16), "deepnorm_ln1")
        f = _moe(h32, hbf, n_tok, router_w[l], router_b[l], w_gu[l], b_gu[l], w_dn[l], b_dn[l])
        (x,) = _residual_ln(h32, f, ln2_g[l], ln2_b[l], alpha, (F32,), "deepnorm_ln2")

        for gi, (win, dil) in enumerate(GROUPS):
            keep = min(win, seq)
            kv_cols = p_b[:, gi * QKV_B + hw:(gi + 1) * QKV_B]
            kv_p[gi].append(kv_cols[:n_p].reshape(batch, seq, 2, H_B, DH_B)[:, seq - keep:])
            new_row = kv_cols[n_p:n_tok].reshape(n_s, 1, 2, H_B, DH_B)
            kv_s[gi].append(jnp.concatenate([kv_caches[gi][l][:, 1:], new_row], axis=1))
        conv_p.append(p_a[:n_p, :CONV_DIM].reshape(batch, seq, CONV_DIM)[:, seq - (CONV_W - 1):])
        conv_s.append(cs)
        delta_p.append(s_p)
        delta_s.append(s_s)

    return (x[:n_p].reshape(batch, seq, d), x[n_p:n_tok].reshape(n_s, 1, d),
            jnp.stack(kv_p[0]), jnp.stack(kv_s[0]),
            jnp.stack(kv_p[1]), jnp.stack(kv_s[1]),
            jnp.stack(kv_p[2]), jnp.stack(kv_s[2]),
            jnp.stack(conv_p), jnp.stack(conv_s),
            jnp.stack(delta_p), jnp.stack(delta_s))
```

```python
import functools
import math

import jax
import jax.numpy as jnp
from jax import lax
from jax.experimental import pallas as pl
from jax.experimental.pallas import tpu as pltpu

F32 = jnp.float32
BF16 = jnp.bfloat16

D_MODEL = 4096
H_A, DK_A, DV_A = 16, 128, 128
CONV_W = 4
CONV_DIM = H_A * (2 * DK_A + DV_A)
CHUNK = 64
H_B, DH_B = 8, 128
GROUPS = ((128, 1), (512, 4), (2048, 16))
N_GROUPS = len(GROUPS)
Q_BLOCK = 128
NUM_BUCKETS, MAX_DISTANCE = 32, 2048
N_EXPERTS, TOP_K = 32, 4
D_FF = D_MODEL
SWIGLU_LIMIT, SWIGLU_ALPHA = 7.0, 1.702
LN_EPS, RMS_EPS = 1e-5, 1e-6

OFF_Z = CONV_DIM
OFF_BETA = OFF_Z + H_A * DV_A
OFF_ALPHA = OFF_BETA + H_A
OFF_B = OFF_ALPHA + H_A
QKV_B = 3 * H_B * DH_B
OFF_GATE = OFF_B + N_GROUPS * QKV_B

LANES = 128
VMEM_LIMIT = 56 * 1024 * 1024
ROW_TILE = 1032
MOE_BLK = 512
MOE_SUB = 128


def _cparams(sem):
    return pltpu.CompilerParams(dimension_semantics=sem, vmem_limit_bytes=VMEM_LIMIT)


def _dot(a, b):
    return jnp.dot(a.astype(BF16), b.astype(BF16), preferred_element_type=F32)


def _dot_nt(a, b):
    return lax.dot_general(a.astype(BF16), b.astype(BF16), (((1,), (1,)), ((), ())),
                           preferred_element_type=F32)


def _dot_tn(a, b):
    return lax.dot_general(a.astype(BF16), b.astype(BF16), (((0,), (0,)), ((), ())),
                           preferred_element_type=F32)


def _dot_hi(a, b):
    return jnp.dot(a, b, precision=lax.Precision.HIGHEST, preferred_element_type=F32)


def _silu(x):
    return x * jax.nn.sigmoid(x)


def _mm_body(*refs, sigmoid_bias, w_is_nk):
    if sigmoid_bias:
        x_ref, w_ref, b_ref, o_ref = refs
    else:
        x_ref, w_ref, o_ref = refs
    w = w_ref[0].astype(BF16)
    contract = (((1,), (1,)), ((), ())) if w_is_nk else (((1,), (0,)), ((), ()))
    acc = lax.dot_general(x_ref[...], w, contract, preferred_element_type=F32)
    if sigmoid_bias:
        acc = jax.nn.sigmoid(acc + b_ref[...])
    o_ref[...] = acc.astype(o_ref.dtype)


def _mm(x, w, *, layer=0, n_cols, col0=0, tm, tn, w_is_nk=False, bias=None, out_dtype=F32, name):
    m, k = x.shape
    if w_is_nk:
        w_spec = pl.BlockSpec((pl.Element(1), pl.Element(tn), pl.Element(k)),
                              lambda i, j: (layer, pl.multiple_of(col0 + j * tn, 8), 0))
    else:
        w_spec = pl.BlockSpec((1, k, tn), lambda i, j: (layer, 0, j + col0 // tn))
    in_specs = [pl.BlockSpec((tm, k), lambda i, j: (i, 0)), w_spec]
    args = [x, w]
    if bias is not None:
        in_specs.append(pl.BlockSpec((1, tn), lambda i, j: (0, j)))
        args.append(bias)
    return pl.pallas_call(
        functools.partial(_mm_body, sigmoid_bias=bias is not None, w_is_nk=w_is_nk),
        grid=(m // tm, n_cols // tn),
        in_specs=in_specs,
        out_specs=pl.BlockSpec((tm, tn), lambda i, j: (i, j)),
        out_shape=jax.ShapeDtypeStruct((m, n_cols), out_dtype),
        compiler_params=_cparams(("parallel", "parallel")),
        name=name,
    )(*args)


def _bg_body(p_ref, a_ref, dt_ref, beta_ref, g_ref, gc_ref):
    p = p_ref[...]
    tt = p.shape[0]
    beta_ref[...] = jax.nn.sigmoid(p)
    g = -jnp.exp(a_ref[...]) * jax.nn.softplus(p + dt_ref[...])
    g_ref[...] = g
    ri = lax.broadcasted_iota(jnp.int32, (tt, tt), 0)
    ci = lax.broadcasted_iota(jnp.int32, (tt, tt), 1)
    same_chunk_before = ((ri // CHUNK) == (ci // CHUNK)) & (ci <= ri)
    gc_ref[...] = _dot_hi(same_chunk_before.astype(F32), g)


def _beta_g(p_ba, a_log_l, dt_bias_l):
    rows = p_ba.shape[0]
    tt = 3 * CHUNK
    zeros = jnp.zeros((LANES - 2 * H_A,), F32)
    a_vec = jnp.concatenate([jnp.zeros((H_A,), F32), a_log_l, zeros]).reshape(1, LANES)
    dt_vec = jnp.concatenate([jnp.zeros((H_A,), F32), dt_bias_l, zeros]).reshape(1, LANES)
    row_spec = pl.BlockSpec((tt, LANES), lambda i: (i, 0))
    vec_spec = pl.BlockSpec((1, LANES), lambda i: (0, 0))
    shp = jax.ShapeDtypeStruct((rows, LANES), F32)
    beta, g, gc = pl.pallas_call(
        _bg_body, grid=(rows // tt,),
        in_specs=[row_spec, vec_spec, vec_spec],
        out_specs=[row_spec, row_spec, row_spec],
        out_shape=[shp, shp, shp],
        compiler_params=_cparams(("parallel",)),
        name="deltanet_gates",
    )(p_ba, a_vec, dt_vec)
    return beta[:, :H_A], g[:, H_A:2 * H_A], gc[:, H_A:2 * H_A]


def _conv_body(cur_ref, halo_ref, w_ref, o_ref, *, seq, tt):
    i = pl.program_id(0)
    j = pl.program_id(1)
    cur = cur_ref[...]
    halo = jnp.where((i * tt) % seq == 0, 0.0, halo_ref[...])
    xh = jnp.concatenate([halo, cur], axis=0)
    w = w_ref[...]
    acc = xh[5:5 + tt] * w[0:1]
    for t in range(1, CONV_W):
        acc = acc + xh[5 + t:5 + t + tt] * w[t:t + 1]
    y = _silu(acc)
    ct = y.shape[1]
    n_qk_blocks = 2 * H_A * DK_A // ct
    is_q = j < H_A * DK_A // ct
    is_qk = j < n_qk_blocks
    scale = jnp.where(is_q, DK_A ** -0.5, 1.0)
    for h in range(ct // DK_A):
        sl = slice(h * DK_A, (h + 1) * DK_A)
        yh = y[:, sl]
        nrm = yh * lax.rsqrt(jnp.sum(yh * yh, axis=-1, keepdims=True) + RMS_EPS) * scale
        o_ref[:, sl] = jnp.where(is_qk, nrm, yh)


def _conv_qkv_prompt(p_a, conv_w_l, n_rows, seq):
    tt, ct = 256, 1024
    return pl.pallas_call(
        functools.partial(_conv_body, seq=seq, tt=tt),
        grid=(n_rows // tt, CONV_DIM // ct),
        in_specs=[pl.BlockSpec((tt, ct), lambda i, j: (i, j)),
                  pl.BlockSpec((8, ct), lambda i, j: (jnp.maximum(i * (tt // 8) - 1, 0), j)),
                  pl.BlockSpec((CONV_W, ct), lambda i, j: (0, j))],
        out_specs=pl.BlockSpec((tt, ct), lambda i, j: (i, j)),
        out_shape=jax.ShapeDtypeStruct((n_rows, CONV_DIM), F32),
        compiler_params=_cparams(("parallel", "parallel")),
        name="conv_silu_l2norm",
    )(p_a, p_a, conv_w_l)


def _split_bf16(a):
    hi = a.astype(BF16)
    return hi, (a - hi.astype(F32)).astype(BF16)


def _dot_3pass(a, b):
    a_hi, a_lo = _split_bf16(a)
    b_hi, b_lo = _split_bf16(b)
    return (jnp.dot(a_hi, b_hi, preferred_element_type=F32) + jnp.dot(a_hi, b_lo, preferred_element_type=F32)
            + jnp.dot(a_lo, b_hi, preferred_element_type=F32))


def _gdn_wy_body(q_ref, k_ref, v_ref, b_ref, gc_ref, gr_ref, u_ref, w_ref, qg_ref, kd_ref, attn_ref, *, hb):
    C = CHUNK
    ri = lax.broadcasted_iota(jnp.int32, (C, C), 0)
    ci = lax.broadcasted_iota(jnp.int32, (C, C), 1)
    tril = ci <= ri
    strict = ci < ri
    ys, zs = [], []
    for hh in range(hb):
        sl = slice(hh * DK_A, (hh + 1) * DK_A)
        q = q_ref[:, sl]
        k = k_ref[:, sl]
        v = v_ref[:, sl]
        b = b_ref[0, 0, 0][:, hh:hh + 1]
        gc = gc_ref[0, 0, 0][:, hh:hh + 1]
        gr = gr_ref[0, 0, 0][hh:hh + 1, :]
        decay = jnp.where(tril, jnp.exp(jnp.where(tril, gc - gr, 0.0)), 0.0)
        kb = k * b
        eg = jnp.exp(gc)
        ys.append(-jnp.where(strict, _dot_nt(kb, k) * decay, 0.0))
        zs.append(jnp.concatenate([v * b, kb * eg], axis=-1))
        qg_ref[:, sl] = (q * eg).astype(qg_ref.dtype)
        kd_ref[:, sl] = (k * jnp.exp(gc[C - 1:C, :] - gc)).astype(kd_ref.dtype)
        attn_ref[0, hh] = jnp.where(tril, _dot_nt(q, k) * decay, 0.0).astype(attn_ref.dtype)
    for step in range(6):
        for hh in range(hb):
            y, z = ys[hh], zs[hh]
            if step < 5:
                prod = _dot_3pass(y, jnp.concatenate([z, y], axis=-1))
                zs[hh] = z + prod[:, :DV_A + DK_A]
                ys[hh] = prod[:, DV_A + DK_A:]
            else:
                zs[hh] = z + _dot_3pass(y, z)
    for hh in range(hb):
        sl = slice(hh * DK_A, (hh + 1) * DK_A)
        u_ref[:, sl] = zs[hh][:, :DV_A]
        w_ref[:, sl] = zs[hh][:, DV_A:].astype(w_ref.dtype)


def _gdn_scan_body(u_ref, w_ref, qg_ref, kd_ref, attn_ref, gl_ref, z_ref, nw_ref, ya_ref, s_ref):
    c = pl.program_id(1)

    @pl.when(c == 0)
    def _():
        s_ref[...] = jnp.zeros_like(s_ref)

    nw = nw_ref[...]
    v_news, o_states = [], []
    for hh in range(H_A):
        sl = slice(hh * DK_A, (hh + 1) * DK_A)
        s_bf = s_ref[0, hh].astype(BF16)
        v_news.append((u_ref[:, sl] - jnp.dot(w_ref[:, sl], s_bf, preferred_element_type=F32)).astype(BF16))
        o_states.append(jnp.dot(qg_ref[:, sl], s_bf, preferred_element_type=F32))
    for hh in range(H_A):
        sl = slice(hh * DK_A, (hh + 1) * DK_A)
        s = s_ref[0, hh]
        v_bf = v_news[hh]
        o = o_states[hh] + jnp.dot(attn_ref[0, hh], v_bf, preferred_element_type=F32)
        decay_last = jnp.exp(gl_ref[0, 0, hh:hh + 1, :])
        s_ref[0, hh] = s * decay_last + lax.dot_general(kd_ref[:, sl], v_bf, (((0,), (0,)), ((), ())),
                                                         preferred_element_type=F32)
        o = o * lax.rsqrt(jnp.mean(o * o, axis=-1, keepdims=True) + RMS_EPS) * nw
        ya_ref[:, sl] = (o * _silu(z_ref[:, sl])).astype(ya_ref.dtype)


def _gdn_prompt(qkv, p_a, beta, gc, o_norm_w_l, batch, seq):
    hb = 8
    nhb = H_A // hb
    nc = seq // CHUNK
    w = hb * DK_A
    hw = H_A * DK_A
    n = batch * seq

    def per_head_cols(a):
        return a.reshape(batch, nc, CHUNK, nhb, hb).transpose(0, 3, 1, 2, 4)

    beta_c = per_head_cols(beta)
    gc_c = per_head_cols(gc)
    gc_r = gc_c.transpose(0, 1, 2, 4, 3)
    col_spec = pl.BlockSpec((1, 1, 1, CHUNK, hb), lambda b, h, c: (b, h, c, 0, 0))
    row_spec = pl.BlockSpec((1, 1, 1, hb, CHUNK), lambda b, h, c: (b, h, c, 0, 0))

    def tok_spec(col_block0):
        return pl.BlockSpec((CHUNK, w), lambda b, h, c: (b * nc + c, col_block0 + h))

    attn_spec = pl.BlockSpec((1, hb, CHUNK, CHUNK), lambda b, h, c: (b * nc + c, h, 0, 0))
    u, wf, qg, kd, attn = pl.pallas_call(
        functools.partial(_gdn_wy_body, hb=hb),
        grid=(batch, nhb, nc),
        in_specs=[tok_spec(0), tok_spec(hw // w), tok_spec(2 * hw // w), col_spec, col_spec, row_spec],
        out_specs=[tok_spec(0), tok_spec(0), tok_spec(0), tok_spec(0), attn_spec],
        out_shape=[jax.ShapeDtypeStruct((n, hw), F32), jax.ShapeDtypeStruct((n, hw), BF16),
                   jax.ShapeDtypeStruct((n, hw), BF16), jax.ShapeDtypeStruct((n, hw), BF16),
                   jax.ShapeDtypeStruct((batch * nc, H_A, CHUNK, CHUNK), BF16)],
        compiler_params=_cparams(("parallel", "parallel", "parallel")),
        name="gated_delta_wy",
    )(qkv, qkv, qkv, beta_c, gc_c, gc_r)

    g_last = jnp.broadcast_to(gc.reshape(batch, nc, CHUNK, H_A)[:, :, CHUNK - 1, :, None], (batch, nc, H_A, LANES))
    row = pl.BlockSpec((CHUNK, hw), lambda b, c: (b * nc + c, 0))
    ya, s_fin = pl.pallas_call(
        _gdn_scan_body,
        grid=(batch, nc),
        in_specs=[row, row, row, row,
                  pl.BlockSpec((1, H_A, CHUNK, CHUNK), lambda b, c: (b * nc + c, 0, 0, 0)),
                  pl.BlockSpec((1, 1, H_A, LANES), lambda b, c: (b, c, 0, 0)),
                  pl.BlockSpec((CHUNK, hw), lambda b, c: (b * nc + c, OFF_Z // hw)),
                  pl.BlockSpec((1, DV_A), lambda b, c: (0, 0))],
        out_specs=[row, pl.BlockSpec((1, H_A, DK_A, DV_A), lambda b, c: (b, 0, 0, 0))],
        out_shape=[jax.ShapeDtypeStruct((n, hw), BF16),
                   jax.ShapeDtypeStruct((batch, H_A, DK_A, DV_A), F32)],
        compiler_params=_cparams(("parallel", "arbitrary")),
        name="gated_delta_scan",
    )(u, wf, qg, kd, attn, g_last, p_a, o_norm_w_l.reshape(1, DV_A))
    return ya, s_fin


def _gdn_sample_body(pa_ref, cbuf_ref, cw_ref, bg_ref, s_ref, nw_ref, ya_ref, conv_ref, so_ref):
    new = pa_ref[0, :, :CONV_DIM]
    buf = cbuf_ref[0]
    w = cw_ref[...]
    acc = new * w[3:4]
    for t in range(CONV_W - 1):
        acc = acc + buf[t:t + 1] * w[t:t + 1]
    y = _silu(acc)
    conv_ref[0, 0:2, :] = buf[1:3]
    conv_ref[0, 2:3, :] = new
    z = pa_ref[0, :, OFF_Z:OFF_BETA]
    bg = bg_ref[0]
    nw = nw_ref[...]
    ri = lax.broadcasted_iota(jnp.int32, (DK_A, DK_A), 0)
    ci = lax.broadcasted_iota(jnp.int32, (DK_A, DK_A), 1)
    eye = ri == ci

    def as_col(row):
        return jnp.sum(jnp.where(eye, row, 0.0), axis=1, keepdims=True)

    for h in range(H_A):
        qh = y[:, h * DK_A:(h + 1) * DK_A]
        kh = y[:, (H_A + h) * DK_A:(H_A + h + 1) * DK_A]
        vh = y[:, (2 * H_A + h) * DK_A:(2 * H_A + h + 1) * DK_A]
        qh = qh * lax.rsqrt(jnp.sum(qh * qh, axis=-1, keepdims=True) + RMS_EPS) * (DK_A ** -0.5)
        kh = kh * lax.rsqrt(jnp.sum(kh * kh, axis=-1, keepdims=True) + RMS_EPS)
        beta = bg[0:1, h:h + 1]
        g = bg[1:2, h:h + 1]
        k_col = as_col(kh)
        s = s_ref[0, h] * jnp.exp(g)
        kv = jnp.sum(s * k_col, axis=0, keepdims=True)
        s = s + k_col * ((vh - kv) * beta)
        so_ref[0, h] = s
        o = jnp.sum(s * as_col(qh), axis=0, keepdims=True)
        o = o * lax.rsqrt(jnp.mean(o * o, axis=-1, keepdims=True) + RMS_EPS) * nw
        ya_ref[0, :, h * DV_A:(h + 1) * DV_A] = (o * _silu(z[:, h * DV_A:(h + 1) * DV_A])).astype(ya_ref.dtype)


def _gdn_sample(p_a_s, conv_buf, conv_w_l, beta_s, g_s, s_delta, o_norm_w_l):
    n = p_a_s.shape[0]
    bg = jnp.stack([beta_s, g_s], axis=1)
    return pl.pallas_call(
        _gdn_sample_body, grid=(n,),
        in_specs=[pl.BlockSpec((1, 1, OFF_BETA), lambda i: (i, 0, 0)),
                  pl.BlockSpec((1, CONV_W - 1, CONV_DIM), lambda i: (i, 0, 0)),
                  pl.BlockSpec((CONV_W, CONV_DIM), lambda i: (0, 0)),
                  pl.BlockSpec((1, 2, H_A), lambda i: (i, 0, 0)),
                  pl.BlockSpec((1, H_A, DK_A, DV_A), lambda i: (i, 0, 0, 0)),
                  pl.BlockSpec((1, DV_A), lambda i: (0, 0))],
        out_specs=[pl.BlockSpec((1, 1, H_A * DV_A), lambda i: (i, 0, 0)),
                   pl.BlockSpec((1, CONV_W - 1, CONV_DIM), lambda i: (i, 0, 0)),
                   pl.BlockSpec((1, H_A, DK_A, DV_A), lambda i: (i, 0, 0, 0))],
        out_shape=[jax.ShapeDtypeStruct((n, 1, H_A * DV_A), BF16),
                   jax.ShapeDtypeStruct((n, CONV_W - 1, CONV_DIM), F32),
                   jax.ShapeDtypeStruct((n, H_A, DK_A, DV_A), F32)],
        compiler_params=_cparams(("parallel",)),
        name="gated_delta_step",
    )(p_a_s, conv_buf, conv_w_l, bg, s_delta, o_norm_w_l.reshape(1, DV_A))


def _t5_bucket(dist):
    max_exact = NUM_BUCKETS // 2
    d = jnp.maximum(dist, 1).astype(F32)
    large = max_exact + (jnp.log(d / max_exact) / math.log(MAX_DISTANCE / max_exact)
                         * (NUM_BUCKETS - max_exact)).astype(jnp.int32)
    large = jnp.minimum(large, NUM_BUCKETS - 1)
    return jnp.where(dist < max_exact, dist, large)


def _attn_prompt_body(q_ref, kp_ref, kc_ref, vp_ref, vc_ref, bias_ref, o_ref, lse_ref, *, dil, hc):
    n = pl.program_id(1)
    hg = pl.program_id(2)
    kj = lax.broadcasted_iota(jnp.int32, (Q_BLOCK, 2 * Q_BLOCK), 1)
    in_seq = (n > 0) | (kj >= Q_BLOCK)
    units = [(pl.ds(r, Q_BLOCK, stride=dil) if dil > 1 else slice(None), h)
             for r in range(dil) for h in range(hc)]
    scores = []
    for rows, h in units:
        sl = slice(h * DH_B, (h + 1) * DH_B)
        kk = jnp.concatenate([kp_ref[rows, sl], kc_ref[rows, sl]], axis=0)
        scores.append(_dot_nt(q_ref[rows, sl], kk))
    for (rows, h), sc in zip(units, scores):
        sl = slice(h * DH_B, (h + 1) * DH_B)
        vv = jnp.concatenate([vp_ref[rows, sl], vc_ref[rows, sl]], axis=0)
        logits = jnp.where(in_seq, sc * (DH_B ** -0.5) + bias_ref[hg * hc + h], -jnp.inf)
        mx = jnp.max(logits, axis=-1, keepdims=True)
        e = jnp.exp(logits - mx)
        den = jnp.sum(e, axis=-1, keepdims=True)
        o_ref[rows, sl] = _dot(e / den, vv)
        lse_ref[rows, sl] = jnp.broadcast_to(mx + jnp.log(den), (Q_BLOCK, DH_B))


def _attn_prompt(p_b, gi, bias_tab, batch, seq):
    _, dil = GROUPS[gi]
    rb = Q_BLOCK * dil
    nb = seq // rb
    hc = H_B if dil == 1 else 1
    cw = hc * DH_B
    hw = H_B * DH_B

    def spec(which, prev):
        def imap(b, n, h):
            nn = jnp.maximum(n - 1, 0) if prev else n
            return (b * nb + nn, (gi * QKV_B + which * hw) // cw + h)
        return pl.BlockSpec((rb, cw), imap)

    out_spec = pl.BlockSpec((rb, cw), lambda b, n, h: (b * nb + n, h))
    shp = jax.ShapeDtypeStruct((batch * seq, hw), F32)
    return pl.pallas_call(
        functools.partial(_attn_prompt_body, dil=dil, hc=hc), grid=(batch, nb, H_B // hc),
        in_specs=[spec(0, False), spec(1, True), spec(1, False), spec(2, True), spec(2, False),
                  pl.BlockSpec((H_B, Q_BLOCK, 2 * Q_BLOCK), lambda b, n, h: (0, 0, 0))],
        out_specs=[out_spec, out_spec],
        out_shape=[shp, shp],
        compiler_params=_cparams(("parallel", "parallel", "parallel")),
        name=f"dilated_attn_prompt_g{gi}",
    )(p_b, p_b, p_b, p_b, p_b, bias_tab)


def _merge_body(o0, o1, o2, l0, l1, l2, yb_ref):
    ls = [l0[...], l1[...], l2[...]]
    mx = jnp.maximum(jnp.maximum(ls[0], ls[1]), ls[2])
    es = [jnp.exp(l - mx) for l in ls]
    den = es[0] + es[1] + es[2]
    acc = (es[0] / den) * o0[...] + (es[1] / den) * o1[...] + (es[2] / den) * o2[...]
    yb_ref[...] = acc.astype(yb_ref.dtype)


def _merge_groups(outs, lses):
    rows, hw = outs[0].shape
    tt = 512
    spec = pl.BlockSpec((tt, hw), lambda i: (i, 0))
    return pl.pallas_call(
        _merge_body, grid=(rows // tt,),
        in_specs=[spec] * 6, out_specs=spec,
        out_shape=jax.ShapeDtypeStruct((rows, hw), BF16),
        compiler_params=_cparams(("parallel",)),
        name="merge_attention_groups",
    )(*outs, *lses)


def _attn_sample_body(pb_ref, c0_ref, c1_ref, c2_ref, bc_ref, bn_ref, yb_ref):
    caches = (c0_ref, c1_ref, c2_ref)
    scale = DH_B ** -0.5
    outs, lses = [], []
    for gi in range(N_GROUPS):
        q = pb_ref[0, 3 * gi]
        k_new = pb_ref[0, 3 * gi + 1]
        v_new = pb_ref[0, 3 * gi + 2]
        kc = caches[gi][0, :, 0, 0]
        vc = caches[gi][0, :, 0, 1]
        lc = jnp.sum(kc * q[None], axis=-1, keepdims=True) * scale + bc_ref[gi]
        ln = jnp.sum(k_new * q, axis=-1, keepdims=True) * scale + bn_ref[gi]
        mx = jnp.maximum(jnp.max(lc, axis=0), ln)
        ec = jnp.exp(lc - mx[None])
        en = jnp.exp(ln - mx)
        den = jnp.sum(ec, axis=0) + en
        outs.append((jnp.sum(ec * vc, axis=0) + en * v_new) / den)
        lses.append(mx + jnp.log(den))
    mx = jnp.maximum(jnp.maximum(lses[0], lses[1]), lses[2])
    es = [jnp.exp(l - mx) for l in lses]
    den = es[0] + es[1] + es[2]
    acc = (es[0] / den) * outs[0] + (es[1] / den) * outs[1] + (es[2] / den) * outs[2]
    yb_ref[0] = acc.astype(yb_ref.dtype)


def _attn_sample(p_b_s, caches, bias_cache, bias_new):
    n = p_b_s.shape[0]
    views, specs = [], []
    for (win, dil), c in zip(GROUPS, caches):
        wb = c.shape[1]
        views.append(c.reshape(n, wb // dil, dil, 2, H_B, DH_B))
        specs.append(pl.BlockSpec((1, wb // dil, 1, 2, H_B, DH_B), lambda i: (i, 0, 0, 0, 0, 0)))
    return pl.pallas_call(
        _attn_sample_body, grid=(n,),
        in_specs=[pl.BlockSpec((1, 3 * N_GROUPS, H_B, DH_B), lambda i: (i, 0, 0, 0))] + specs
                 + [pl.BlockSpec((N_GROUPS, Q_BLOCK, H_B, 1), lambda i: (0, 0, 0, 0)),
                    pl.BlockSpec((N_GROUPS, H_B, 1), lambda i: (0, 0, 0))],
        out_specs=pl.BlockSpec((1, H_B, DH_B), lambda i: (i, 0, 0)),
        out_shape=jax.ShapeDtypeStruct((n, H_B, DH_B), BF16),
        compiler_params=_cparams(("parallel",)),
        name="dilated_attn_sample",
    )(p_b_s, *views, bias_cache, bias_new)


CACHE_COPY_PARTS = 8


def _cache_shift_body(*refs):
    caches, news, outs, sem = refs[0:3], refs[3:6], refs[6:9], refs[9]
    copies = []
    for g in range(N_GROUPS):
        n, wb = caches[g].shape[0], caches[g].shape[1]
        step = n // CACHE_COPY_PARTS
        for part in range(CACHE_COPY_PARTS):
            rows = pl.ds(part * step, step)
            copies.append(pltpu.make_async_copy(caches[g].at[rows, pl.ds(1, wb - 1)],
                                                outs[g].at[rows, pl.ds(0, wb - 1)], sem.at[2 * g]))
        copies.append(pltpu.make_async_copy(news[g], outs[g].at[:, pl.ds(wb - 1, 1)], sem.at[2 * g + 1]))
    for cp in copies:
        cp.start()
    for cp in copies:
        cp.wait()


def _cache_shift(caches, new_rows):
    any_spec = pl.BlockSpec(memory_space=pl.ANY)
    return pl.pallas_call(
        _cache_shift_body,
        in_specs=[any_spec] * (2 * N_GROUPS),
        out_specs=[any_spec] * N_GROUPS,
        out_shape=[jax.ShapeDtypeStruct(c.shape, c.dtype) for c in caches],
        scratch_shapes=[pltpu.SemaphoreType.DMA((2 * N_GROUPS,))],
        name="kv_cache_shift",
    )(*caches, *new_rows)


def _branch_body(ya_ref, yb_ref, wa_ref, wb_ref, ga_ref, gb_ref, o_ref):
    ba = jnp.dot(ya_ref[...], wa_ref[...].astype(BF16), preferred_element_type=F32)
    bb = jnp.dot(yb_ref[...], wb_ref[...].astype(BF16), preferred_element_type=F32)
    o_ref[...] = (ga_ref[...] * ba + gb_ref[...] * bb).astype(o_ref.dtype)


def _branch_merge(ya, yb, w_pa, w_pb, gates):
    rows = ya.shape[0]
    tm, tn = ROW_TILE, 512
    nj = D_MODEL // tn
    return pl.pallas_call(
        _branch_body, grid=(rows // tm, nj),
        in_specs=[pl.BlockSpec((tm, ya.shape[1]), lambda i, j: (i, 0)),
                  pl.BlockSpec((tm, yb.shape[1]), lambda i, j: (i, 0)),
                  pl.BlockSpec((w_pa.shape[0], tn), lambda i, j: (0, j)),
                  pl.BlockSpec((w_pb.shape[0], tn), lambda i, j: (0, j)),
                  pl.BlockSpec((tm, tn), lambda i, j: (i, j)),
                  pl.BlockSpec((tm, tn), lambda i, j: (i, nj + j))],
        out_specs=pl.BlockSpec((tm, tn), lambda i, j: (i, j)),
        out_shape=jax.ShapeDtypeStruct((rows, D_MODEL), BF16),
        compiler_params=_cparams(("parallel", "parallel")),
        name="branch_proj_gate",
    )(ya, yb, w_pa, w_pb, gates, gates)


def _ln_body(x_ref, r_ref, g_ref, b_ref, *out_refs, alpha):
    v = alpha * x_ref[...] + r_ref[...]
    mu = jnp.mean(v, axis=-1, keepdims=True)
    var = jnp.mean(jnp.square(v - mu), axis=-1, keepdims=True)
    y = (v - mu) * lax.rsqrt(var + LN_EPS) * g_ref[...] + b_ref[...]
    for o_ref in out_refs:
        o_ref[...] = y.astype(o_ref.dtype)


def _residual_ln(x, r, g, b, alpha, out_dtypes, name):
    rows, d = x.shape
    tt = 344
    spec = pl.BlockSpec((tt, d), lambda i: (i, 0))
    vec = pl.BlockSpec((1, d), lambda i: (0, 0))
    return pl.pallas_call(
        functools.partial(_ln_body, alpha=alpha), grid=(rows // tt,),
        in_specs=[spec, spec, vec, vec],
        out_specs=[spec] * len(out_dtypes),
        out_shape=[jax.ShapeDtypeStruct((rows, d), dt) for dt in out_dtypes],
        compiler_params=_cparams(("parallel",)),
        name=name,
    )(x, r, g.reshape(1, d), b.reshape(1, d))


def _router_body(h_ref, w_ref, b_ref, idx_ref, gate_ref):
    logits = _dot(h_ref[...], w_ref[...]) + b_ref[...]
    lane = lax.broadcasted_iota(jnp.int32, logits.shape, 1)
    vals, idxs = [], []
    for _ in range(TOP_K):
        mx = jnp.max(logits, axis=-1, keepdims=True)
        idx = jnp.min(jnp.where(logits == mx, lane, LANES), axis=-1, keepdims=True)
        vals.append(mx)
        idxs.append(idx)
        logits = jnp.where(lane == idx, -jnp.inf, logits)
    es = [jnp.exp(v - vals[0]) for v in vals]
    den = es[0] + es[1] + es[2] + es[3]
    idx_out = jnp.zeros(logits.shape, jnp.int32)
    gate_out = jnp.zeros(logits.shape, F32)
    for k in range(TOP_K):
        idx_out = jnp.where(lane == k, idxs[k], idx_out)
        gate_out = jnp.where(lane == k, es[k] / den, gate_out)
    idx_ref[...] = idx_out
    gate_ref[...] = gate_out


def _router(h, router_w_l, router_b_l):
    rows, d = h.shape
    tt = 344
    w = jnp.pad(router_w_l, ((0, 0), (0, LANES - N_EXPERTS)))
    b = jnp.concatenate([router_b_l, jnp.full((LANES - N_EXPERTS,), -jnp.inf, F32)]).reshape(1, LANES)
    spec = pl.BlockSpec((tt, LANES), lambda i: (i, 0))
    return pl.pallas_call(
        _router_body, grid=(rows // tt,),
        in_specs=[pl.BlockSpec((tt, d), lambda i: (i, 0)),
                  pl.BlockSpec((d, LANES), lambda i: (0, 0)),
                  pl.BlockSpec((1, LANES), lambda i: (0, 0))],
        out_specs=[spec, spec],
        out_shape=[jax.ShapeDtypeStruct((rows, LANES), jnp.int32),
                   jax.ShapeDtypeStruct((rows, LANES), F32)],
        compiler_params=_cparams(("parallel",)),
        name="router_top4",
    )(h, w, b)


def _gmm1_body(be_ref, first_ref, nsub_ref, x_ref, wg_ref, wl_ref, bg_ref, bl_ref, a_ref, wg_bf, wl_bf):
    i = pl.program_id(1)

    @pl.when(first_ref[i] == 1)
    def _():
        wg_bf[...] = wg_ref[0].astype(BF16)
        wl_bf[...] = wl_ref[0].astype(BF16)

    for s in range(MOE_BLK // MOE_SUB):
        @pl.when(s < nsub_ref[i])
        def _():
            rows = slice(s * MOE_SUB, (s + 1) * MOE_SUB)
            x = x_ref[rows, :]
            hg = jnp.dot(x, wg_bf[...], preferred_element_type=F32) + bg_ref[0]
            hl = jnp.dot(x, wl_bf[...], preferred_element_type=F32) + bl_ref[0]
            glu = jnp.minimum(hg, SWIGLU_LIMIT)
            lin = jnp.clip(hl, -SWIGLU_LIMIT, SWIGLU_LIMIT)
            a_ref[rows, :] = (glu * jax.nn.sigmoid(SWIGLU_ALPHA * glu) * (lin + 1.0)).astype(a_ref.dtype)

        @pl.when(s >= nsub_ref[i])
        def _():
            a_ref[s * MOE_SUB:(s + 1) * MOE_SUB, :] = jnp.zeros((MOE_SUB, a_ref.shape[1]), a_ref.dtype)


def _gmm1(xs, w_gu_l, b_gu_l, block_e, first, nsub):
    n_slots, d = xs.shape
    nb = n_slots // MOE_BLK
    tn = 512
    nj = D_FF // tn
    b3 = b_gu_l.reshape(N_EXPERTS, 1, 2 * D_FF)
    grid_spec = pltpu.PrefetchScalarGridSpec(
        num_scalar_prefetch=3, grid=(nj, nb),
        in_specs=[pl.BlockSpec((MOE_BLK, d), lambda j, i, be, fs, ns: (i, 0)),
                  pl.BlockSpec((1, d, tn), lambda j, i, be, fs, ns: (be[i], 0, j)),
                  pl.BlockSpec((1, d, tn), lambda j, i, be, fs, ns: (be[i], 0, nj + j)),
                  pl.BlockSpec((1, 1, tn), lambda j, i, be, fs, ns: (be[i], 0, j)),
                  pl.BlockSpec((1, 1, tn), lambda j, i, be, fs, ns: (be[i], 0, nj + j))],
        out_specs=pl.BlockSpec((MOE_BLK, tn), lambda j, i, be, fs, ns: (i, j)),
        scratch_shapes=[pltpu.VMEM((d, tn), BF16), pltpu.VMEM((d, tn), BF16)])
    return pl.pallas_call(
        _gmm1_body, grid_spec=grid_spec,
        out_shape=jax.ShapeDtypeStruct((n_slots, D_FF), BF16),
        compiler_params=_cparams(("arbitrary", "arbitrary")),
        name="moe_up_swiglu",
    )(block_e, first, nsub, xs, w_gu_l, w_gu_l, b3, b3)


def _gmm2_body(be_ref, first_ref, nsub_ref, a_ref, w_ref, b_ref, y_ref, w_bf):
    i = pl.program_id(1)

    @pl.when(first_ref[i] == 1)
    def _():
        w_bf[...] = w_ref[0].astype(BF16)

    for s in range(MOE_BLK // MOE_SUB):
        @pl.when(s < nsub_ref[i])
        def _():
            rows = slice(s * MOE_SUB, (s + 1) * MOE_SUB)
            y_ref[rows, :] = jnp.dot(a_ref[rows, :], w_bf[...], preferred_element_type=F32) + b_ref[0]

        @pl.when(s >= nsub_ref[i])
        def _():
            y_ref[s * MOE_SUB:(s + 1) * MOE_SUB, :] = jnp.zeros((MOE_SUB, y_ref.shape[1]), y_ref.dtype)


def _gmm2(a, w_dn_l, b_dn_l, block_e, first, nsub):
    n_slots, f = a.shape
    nb = n_slots // MOE_BLK
    tn = min(1024, D_MODEL)
    nj = D_MODEL // tn
    b3 = b_dn_l.reshape(N_EXPERTS, 1, D_MODEL)
    grid_spec = pltpu.PrefetchScalarGridSpec(
        num_scalar_prefetch=3, grid=(nj, nb),
        in_specs=[pl.BlockSpec((MOE_BLK, f), lambda j, i, be, fs, ns: (i, 0)),
                  pl.BlockSpec((1, f, tn), lambda j, i, be, fs, ns: (be[i], 0, j)),
                  pl.BlockSpec((1, 1, tn), lambda j, i, be, fs, ns: (be[i], 0, j))],
        out_specs=pl.BlockSpec((MOE_BLK, tn), lambda j, i, be, fs, ns: (i, j)),
        scratch_shapes=[pltpu.VMEM((f, tn), BF16)])
    return pl.pallas_call(
        _gmm2_body, grid_spec=grid_spec,
        out_shape=jax.ShapeDtypeStruct((n_slots, D_MODEL), F32),
        compiler_params=_cparams(("arbitrary", "arbitrary")),
        name="moe_down",
    )(block_e, first, nsub, a, w_dn_l, b3)


COMBINE_TILE = 192


def _combine_row_copy(y_hbm, ybuf, sem, slot, t, k):
    return pltpu.make_async_copy(y_hbm.at[pl.ds(slot, 1)], ybuf.at[k, pl.ds(t, 1)], sem.at[k])


def _combine_ln_body(dest_ref, y_hbm, gate_ref, h_ref, g_ref, b_ref, o_ref, ybuf, sem, *, alpha):
    tt = h_ref.shape[0]

    def start_rows(t, carry):
        for k in range(TOP_K):
            _combine_row_copy(y_hbm, ybuf, sem, dest_ref[0, 0, t * TOP_K + k], t, k).start()
        return carry

    lax.fori_loop(0, tt, start_rows, 0)

    def wait_rows(t, carry):
        for k in range(TOP_K):
            _combine_row_copy(y_hbm, ybuf, sem, 0, t, k).wait()
        return carry

    lax.fori_loop(0, tt, wait_rows, 0)
    gate = gate_ref[...]
    f = gate[:, 0:1] * ybuf[0]
    for k in range(1, TOP_K):
        f = f + gate[:, k:k + 1] * ybuf[k]
    v = alpha * h_ref[...] + f
    mu = jnp.mean(v, axis=-1, keepdims=True)
    var = jnp.mean(jnp.square(v - mu), axis=-1, keepdims=True)
    o_ref[...] = (v - mu) * lax.rsqrt(var + LN_EPS) * g_ref[...] + b_ref[...]


def _combine_ln(h32, y, dest, gate_pad, g, b, alpha):
    rows, d = h32.shape
    tt = COMBINE_TILE
    nt = rows // tt
    row = pl.BlockSpec((tt, d), lambda i: (i, 0))
    vec = pl.BlockSpec((1, d), lambda i: (0, 0))
    return pl.pallas_call(
        functools.partial(_combine_ln_body, alpha=alpha), grid=(nt,),
        in_specs=[pl.BlockSpec((1, 1, tt * TOP_K), lambda i: (i, 0, 0), memory_space=pltpu.SMEM),
                  pl.BlockSpec(memory_space=pl.ANY),
                  pl.BlockSpec((tt, LANES), lambda i: (i, 0)), row, vec, vec],
        out_specs=row,
        out_shape=jax.ShapeDtypeStruct((rows, d), F32),
        scratch_shapes=[pltpu.VMEM((TOP_K, tt, d), F32), pltpu.SemaphoreType.DMA((TOP_K,))],
        compiler_params=_cparams(("arbitrary",)),
        name="moe_combine_ln2",
    )(dest.reshape(nt, 1, tt * TOP_K), y, gate_pad, h32, g.reshape(1, d), b.reshape(1, d))


def _moe(h32, hbf, n_tok, router_w_l, router_b_l, w_gu_l, b_gu_l, w_dn_l, b_dn_l):
    rows = h32.shape[0]
    idx_pad, gate_pad = _router(h32, router_w_l, router_b_l)
    top_idx = idx_pad[:n_tok, :TOP_K]
    gate = gate_pad[:n_tok, :TOP_K]
    n_assign = n_tok * TOP_K
    n_blocks = -(-n_assign // MOE_BLK) + N_EXPERTS
    n_slots = n_blocks * MOE_BLK
    flat_e = top_idx.reshape(-1)
    onehot = (flat_e[:, None] == jnp.arange(N_EXPERTS, dtype=jnp.int32)[None, :]).astype(jnp.int32)
    counts = jnp.sum(onehot, axis=0)
    padded = (counts + MOE_BLK - 1) // MOE_BLK * MOE_BLK
    pad_end = jnp.cumsum(padded)
    pad_start = pad_end - padded
    dest = jnp.sum((jnp.cumsum(onehot, axis=0) - onehot + pad_start[None, :]) * onehot, axis=1)
    slot_tok = jnp.zeros((n_slots,), jnp.int32).at[dest].set(jnp.arange(n_assign, dtype=jnp.int32) // TOP_K)
    n_used = pad_end[-1] // MOE_BLK
    blk = jnp.minimum(jnp.arange(n_blocks, dtype=jnp.int32), n_used - 1)
    block_e = jnp.minimum(jnp.sum((pad_end[None, :] <= (blk * MOE_BLK)[:, None]).astype(jnp.int32), axis=1),
                          N_EXPERTS - 1)
    first = jnp.concatenate([jnp.ones((1,), jnp.int32), (block_e[1:] != block_e[:-1]).astype(jnp.int32)])
    seg_rows = jnp.clip(counts[block_e] - (blk * MOE_BLK - pad_start[block_e]), 0, MOE_BLK)
    nsub = jnp.where(jnp.arange(n_blocks) < n_used, (seg_rows + MOE_SUB - 1) // MOE_SUB, 0).astype(jnp.int32)
    xs = hbf[slot_tok]
    a = _gmm1(xs, w_gu_l, b_gu_l, block_e, first, nsub)
    y = _gmm2(a, w_dn_l, b_dn_l, block_e, first, nsub)
    dest_pad = jnp.pad(dest.reshape(n_tok, TOP_K), ((0, rows - n_tok), (0, 0)))
    row_is_token = (jnp.arange(rows) < n_tok)[:, None]
    return y, dest_pad, jnp.where(row_is_token, gate_pad, 0.0)


def _attention_bias_tables(rel_bias):
    qi = jnp.arange(Q_BLOCK)[:, None]
    kj = jnp.arange(2 * Q_BLOCK)[None, :]
    rel = qi + Q_BLOCK - kj
    prompt_tabs, cache_tabs, new_tabs = [], [], []
    buckets = jnp.arange(NUM_BUCKETS, dtype=jnp.int32)
    for gi, (win, dil) in enumerate(GROUPS):
        steps = win // dil
        bias_g = rel_bias[:, gi * H_B:(gi + 1) * H_B].astype(F32)
        hit = _t5_bucket(jnp.maximum(rel, 0) * dil)[None, :, :, None] == buckets
        tab = jnp.sum(jnp.where(hit, bias_g.T[:, None, None, :], 0.0), axis=-1)
        prompt_tabs.append(jnp.where(((rel >= 0) & (rel <= steps))[None], tab, -jnp.inf))
        bj = bias_g[_t5_bucket(jnp.arange(steps + 1) * dil)]
        cache_tabs.append(bj[:0:-1, :, None])
        new_tabs.append(bj[0, :, None])
    return prompt_tabs, jnp.stack(cache_tabs), jnp.stack(new_tabs)


def kernel(x_prompt, x_sample, cache_kv_w128, cache_kv_w512, cache_kv_w2048, state_conv, state_delta, rel_bias,
           w_in, b_gate, conv_w, a_log, dt_bias, o_norm_w, w_branch_a, w_branch_b, w_out,
           ln1_g, ln1_b, router_w, router_b, w_gu, b_gu, w_dn, b_dn, ln2_g, ln2_b):
    batch, seq, d = x_prompt.shape
    n_s = x_sample.shape[0]
    n_p = batch * seq
    n_tok = n_p + n_s
    rows = 8 * ROW_TILE
    depth = w_in.shape[0]
    alpha = (2.0 * depth) ** 0.25
    kv_caches = (cache_kv_w128, cache_kv_w512, cache_kv_w2048)
    hw = H_B * DH_B

    x = jnp.concatenate([x_prompt.reshape(n_p, d), x_sample.reshape(n_s, d),
                         jnp.zeros((rows - n_tok, d), F32)], axis=0)
    prompt_tabs, bias_cache, bias_new = _attention_bias_tables(rel_bias)

    kv_p = [[] for _ in GROUPS]
    kv_s = [[] for _ in GROUPS]
    conv_p, conv_s, delta_p, delta_s = [], [], [], []
    for l in range(depth):
        xb = x.astype(BF16)
        w_in_t = jnp.swapaxes(w_in, 1, 2)
        in_proj = functools.partial(_mm, xb, w_in_t, layer=l, tm=ROW_TILE, w_is_nk=True)
        p_a = in_proj(n_cols=OFF_BETA, tn=512, name="in_proj_deltanet")
        p_ba = in_proj(n_cols=LANES, col0=OFF_BETA, tn=LANES, name="in_proj_beta_alpha")
        p_b = in_proj(n_cols=N_GROUPS * QKV_B, col0=OFF_B, tn=512, name="in_proj_attention")
        gates = in_proj(n_cols=2 * D_MODEL, col0=OFF_GATE, tn=512,
                        bias=b_gate[l].reshape(1, 2 * D_MODEL), name="in_proj_gates")

        beta, g, gc = _beta_g(p_ba, a_log[l], dt_bias[l])
        qkv = _conv_qkv_prompt(p_a, conv_w[l], n_p, seq)
        ya_p, s_p = _gdn_prompt(qkv, p_a, beta[:n_p], gc[:n_p], o_norm_w[l], batch, seq)
        ya_s, cs, s_s = _gdn_sample(p_a[n_p:n_tok].reshape(n_s, 1, OFF_BETA), state_conv[l], conv_w[l],
                                    beta[n_p:n_tok], g[n_p:n_tok], state_delta[l], o_norm_w[l])
        ya = jnp.concatenate([ya_p, ya_s.reshape(n_s, H_A * DV_A),
                              jnp.zeros((rows - n_tok, H_A * DV_A), BF16)], axis=0)

        outs, lses = [], []
        for gi in range(N_GROUPS):
            o_g, lse_g = _attn_prompt(p_b, gi, prompt_tabs[gi], batch, seq)
            outs.append(o_g)
            lses.append(lse_g)
        yb_p = _merge_groups(outs, lses)
        p_b_s = p_b[n_p:n_tok]
        yb_s = _attn_sample(p_b_s.reshape(n_s, 3 * N_GROUPS, H_B, DH_B), [c[l] for c in kv_caches],
                            bias_cache, bias_new)
        yb = jnp.concatenate([yb_p, yb_s.reshape(n_s, hw), jnp.zeros((rows - n_tok, hw), BF16)], axis=0)

        merged = _branch_merge(ya, yb, w_branch_a[l], w_branch_b[l], gates)
        mix = _mm(merged, w_out, layer=l, n_cols=D_MODEL, tm=ROW_TILE, tn=512, name="out_proj")
        h32, hbf = _residual_ln(x, mix, ln1_g[l], ln1_b[l], alpha, (F32, BF16), "deepnorm_ln1")
        y_slots, dest, gate = _moe(h32, hbf, n_tok, router_w[l], router_b[l], w_gu[l], b_gu[l], w_dn[l], b_dn[l])
        x = _combine_ln(h32, y_slots, dest, gate, ln2_g[l], ln2_b[l], alpha)

        new_rows = []
        for gi, (win, dil) in enumerate(GROUPS):
            keep = min(win, seq)
            c0, c1 = gi * QKV_B + hw, (gi + 1) * QKV_B
            last = [p_b[(b + 1) * seq - keep:(b + 1) * seq, c0:c1] for b in range(batch)]
            kv_p[gi].append(jnp.stack(last).reshape(batch, keep, 2, H_B, DH_B))
            new_rows.append(p_b_s[:, c0:c1].reshape(n_s, 1, 2, H_B, DH_B))
        for gi, shifted in enumerate(_cache_shift([c[l] for c in kv_caches], new_rows)):
            kv_s[gi].append(shifted)
        conv_p.append(jnp.stack([p_a[(b + 1) * seq - (CONV_W - 1):(b + 1) * seq, :CONV_DIM] for b in range(batch)]))
        conv_s.append(cs)
        delta_p.append(s_p)
        delta_s.append(s_s)

    return (x[:n_p].reshape(batch, seq, d), x[n_p:n_tok].reshape(n_s, 1, d),
            jnp.stack(kv_p[0]), jnp.stack(kv_s[0]),
            jnp.stack(kv_p[1]), jnp.stack(kv_s[1]),
            jnp.stack(kv_p[2]), jnp.stack(kv_s[2]),
            jnp.stack(conv_p), jnp.stack(conv_s),
            jnp.stack(delta_p), jnp.stack(delta_s))
```

```python
import functools
import math

import jax
import jax.numpy as jnp
from jax import lax
from jax.experimental import pallas as pl
from jax.experimental.pallas import tpu as pltpu

F32 = jnp.float32
BF16 = jnp.bfloat16

D_MODEL = 4096
H_A, DK_A, DV_A = 16, 128, 128
CONV_W = 4
CONV_DIM = H_A * (2 * DK_A + DV_A)
CHUNK = 64
H_B, DH_B = 8, 128
GROUPS = ((128, 1), (512, 4), (2048, 16))
N_GROUPS = len(GROUPS)
Q_BLOCK = 128
NUM_BUCKETS, MAX_DISTANCE = 32, 2048
N_EXPERTS, TOP_K = 32, 4
D_FF = D_MODEL
SWIGLU_LIMIT, SWIGLU_ALPHA = 7.0, 1.702
LN_EPS, RMS_EPS = 1e-5, 1e-6

OFF_Z = CONV_DIM
OFF_BETA = OFF_Z + H_A * DV_A
OFF_ALPHA = OFF_BETA + H_A
OFF_B = OFF_ALPHA + H_A
QKV_B = 3 * H_B * DH_B
OFF_GATE = OFF_B + N_GROUPS * QKV_B

LANES = 128
VMEM_LIMIT = 56 * 1024 * 1024
ROW_TILE = 1032
MOE_BLK = 512
MOE_SUB = 128


def _cparams(sem):
    return pltpu.CompilerParams(dimension_semantics=sem, vmem_limit_bytes=VMEM_LIMIT)


def _dot(a, b):
    return jnp.dot(a.astype(BF16), b.astype(BF16), preferred_element_type=F32)


def _dot_nt(a, b):
    return lax.dot_general(a.astype(BF16), b.astype(BF16), (((1,), (1,)), ((), ())),
                           preferred_element_type=F32)


def _dot_tn(a, b):
    return lax.dot_general(a.astype(BF16), b.astype(BF16), (((0,), (0,)), ((), ())),
                           preferred_element_type=F32)


def _dot_hi(a, b):
    return jnp.dot(a, b, precision=lax.Precision.HIGHEST, preferred_element_type=F32)


def _silu(x):
    return x * jax.nn.sigmoid(x)


def _rb(x):
    return x.astype(BF16).astype(F32)


def _mm_body(*refs, sigmoid_bias, w_is_nk):
    if sigmoid_bias:
        x_ref, w_ref, b_ref, o_ref = refs
    else:
        x_ref, w_ref, o_ref = refs
    w = w_ref[0].astype(BF16)
    contract = (((1,), (1,)), ((), ())) if w_is_nk else (((1,), (0,)), ((), ()))
    acc = lax.dot_general(x_ref[...], w, contract, preferred_element_type=F32)
    if sigmoid_bias:
        acc = jax.nn.sigmoid(acc + b_ref[...])
    o_ref[...] = acc.astype(o_ref.dtype)


def _mm(x, w, *, layer=0, n_cols, col0=0, tm, tn, w_is_nk=False, bias=None, out_dtype=F32, name):
    m, k = x.shape
    if w_is_nk:
        w_spec = pl.BlockSpec((pl.Element(1), pl.Element(tn), pl.Element(k)),
                              lambda i, j: (layer, pl.multiple_of(col0 + j * tn, 8), 0))
    else:
        w_spec = pl.BlockSpec((1, k, tn), lambda i, j: (layer, 0, j + col0 // tn))
    in_specs = [pl.BlockSpec((tm, k), lambda i, j: (i, 0)), w_spec]
    args = [x, w]
    if bias is not None:
        in_specs.append(pl.BlockSpec((1, tn), lambda i, j: (0, j)))
        args.append(bias)
    return pl.pallas_call(
        functools.partial(_mm_body, sigmoid_bias=bias is not None, w_is_nk=w_is_nk),
        grid=(m // tm, n_cols // tn),
        in_specs=in_specs,
        out_specs=pl.BlockSpec((tm, tn), lambda i, j: (i, j)),
        out_shape=jax.ShapeDtypeStruct((m, n_cols), out_dtype),
        compiler_params=_cparams(("parallel", "parallel")),
        name=name,
    )(*args)


def _bg_body(p_ref, a_ref, dt_ref, beta_ref, g_ref, gc_ref):
    p = p_ref[...]
    tt = p.shape[0]
    beta_ref[...] = jax.nn.sigmoid(p)
    g = -jnp.exp(a_ref[...]) * jax.nn.softplus(p + dt_ref[...])
    g_ref[...] = g
    ri = lax.broadcasted_iota(jnp.int32, (tt, tt), 0)
    ci = lax.broadcasted_iota(jnp.int32, (tt, tt), 1)
    same_chunk_before = ((ri // CHUNK) == (ci // CHUNK)) & (ci <= ri)
    gc_ref[...] = _dot_hi(same_chunk_before.astype(F32), g)


def _beta_g(p_ba, a_log_l, dt_bias_l):
    rows = p_ba.shape[0]
    tt = 3 * CHUNK
    zeros = jnp.zeros((LANES - 2 * H_A,), F32)
    a_vec = jnp.concatenate([jnp.zeros((H_A,), F32), a_log_l, zeros]).reshape(1, LANES)
    dt_vec = jnp.concatenate([jnp.zeros((H_A,), F32), dt_bias_l, zeros]).reshape(1, LANES)
    row_spec = pl.BlockSpec((tt, LANES), lambda i: (i, 0))
    vec_spec = pl.BlockSpec((1, LANES), lambda i: (0, 0))
    shp = jax.ShapeDtypeStruct((rows, LANES), F32)
    beta, g, gc = pl.pallas_call(
        _bg_body, grid=(rows // tt,),
        in_specs=[row_spec, vec_spec, vec_spec],
        out_specs=[row_spec, row_spec, row_spec],
        out_shape=[shp, shp, shp],
        compiler_params=_cparams(("parallel",)),
        name="deltanet_gates",
    )(p_ba, a_vec, dt_vec)
    return beta[:, :H_A], g[:, H_A:2 * H_A], gc[:, H_A:2 * H_A]


def _conv_body(cur_ref, halo_ref, w_ref, o_ref, *, seq, tt):
    i = pl.program_id(0)
    j = pl.program_id(1)
    cur = cur_ref[...]
    halo = jnp.where((i * tt) % seq == 0, 0.0, halo_ref[...])
    xh = jnp.concatenate([halo, cur], axis=0)
    w = w_ref[...]
    acc = xh[5:5 + tt] * w[0:1]
    for t in range(1, CONV_W):
        acc = acc + xh[5 + t:5 + t + tt] * w[t:t + 1]
    y = _silu(acc)
    ct = y.shape[1]
    n_qk_blocks = 2 * H_A * DK_A // ct
    is_q = j < H_A * DK_A // ct
    is_qk = j < n_qk_blocks
    scale = jnp.where(is_q, DK_A ** -0.5, 1.0)
    for h in range(ct // DK_A):
        sl = slice(h * DK_A, (h + 1) * DK_A)
        yh = y[:, sl]
        nrm = yh * lax.rsqrt(jnp.sum(yh * yh, axis=-1, keepdims=True) + RMS_EPS) * scale
        o_ref[:, sl] = jnp.where(is_qk, nrm, yh)


def _conv_qkv_prompt(p_a, conv_w_l, n_rows, seq):
    tt, ct = 256, 1024
    return pl.pallas_call(
        functools.partial(_conv_body, seq=seq, tt=tt),
        grid=(n_rows // tt, CONV_DIM // ct),
        in_specs=[pl.BlockSpec((tt, ct), lambda i, j: (i, j)),
                  pl.BlockSpec((8, ct), lambda i, j: (jnp.maximum(i * (tt // 8) - 1, 0), j)),
                  pl.BlockSpec((CONV_W, ct), lambda i, j: (0, j))],
        out_specs=pl.BlockSpec((tt, ct), lambda i, j: (i, j)),
        out_shape=jax.ShapeDtypeStruct((n_rows, CONV_DIM), F32),
        compiler_params=_cparams(("parallel", "parallel")),
        name="conv_silu_l2norm",
    )(p_a, p_a, conv_w_l)


def _split_bf16(a):
    hi = a.astype(BF16)
    return hi, (a - hi.astype(F32)).astype(BF16)


def _dot_3pass(a, b):
    a_hi, a_lo = _split_bf16(a)
    b_hi, b_lo = _split_bf16(b)
    return (jnp.dot(a_hi, b_hi, preferred_element_type=F32) + jnp.dot(a_hi, b_lo, preferred_element_type=F32)
            + jnp.dot(a_lo, b_hi, preferred_element_type=F32))


def _gdn_wy_body(q_ref, k_ref, v_ref, b_ref, gc_ref, gr_ref, u_ref, w_ref, qg_ref, kd_ref, attn_ref, *, hb):
    C = CHUNK
    ri = lax.broadcasted_iota(jnp.int32, (C, C), 0)
    ci = lax.broadcasted_iota(jnp.int32, (C, C), 1)
    tril = ci <= ri
    strict = ci < ri
    ys, zs = [], []
    for hh in range(hb):
        sl = slice(hh * DK_A, (hh + 1) * DK_A)
        q = q_ref[:, sl]
        k = k_ref[:, sl]
        v = v_ref[:, sl]
        b = b_ref[0, 0, 0][:, hh:hh + 1]
        gc = gc_ref[0, 0, 0][:, hh:hh + 1]
        gr = gr_ref[0, 0, 0][hh:hh + 1, :]
        decay = jnp.where(tril, jnp.exp(jnp.where(tril, gc - gr, 0.0)), 0.0)
        kb = k * b
        eg = jnp.exp(gc)
        ys.append(-jnp.where(strict, _dot_nt(kb, k) * decay, 0.0))
        zs.append(jnp.concatenate([v * b, kb * eg], axis=-1))
        qg_ref[:, sl] = (q * eg).astype(qg_ref.dtype)
        kd_ref[:, sl] = (k * jnp.exp(gc[C - 1:C, :] - gc)).astype(kd_ref.dtype)
        attn_ref[0, hh] = jnp.where(tril, _dot_nt(q, k) * decay, 0.0).astype(attn_ref.dtype)
    for step in range(6):
        for hh in range(hb):
            y, z = ys[hh], zs[hh]
            if step < 5:
                prod = _dot_3pass(y, jnp.concatenate([z, y], axis=-1))
                zs[hh] = z + prod[:, :DV_A + DK_A]
                ys[hh] = prod[:, DV_A + DK_A:]
            else:
                zs[hh] = z + _dot_3pass(y, z)
    for hh in range(hb):
        sl = slice(hh * DK_A, (hh + 1) * DK_A)
        u_ref[:, sl] = zs[hh][:, :DV_A]
        w_ref[:, sl] = zs[hh][:, DV_A:].astype(w_ref.dtype)


def _gdn_scan_body(u_ref, w_ref, qg_ref, kd_ref, attn_ref, gl_ref, z_ref, nw_ref, ya_ref, s_ref):
    c = pl.program_id(1)

    @pl.when(c == 0)
    def _():
        s_ref[...] = jnp.zeros_like(s_ref)

    nw = nw_ref[...]
    v_news, o_states = [], []
    for hh in range(H_A):
        sl = slice(hh * DK_A, (hh + 1) * DK_A)
        s_bf = s_ref[0, hh].astype(BF16)
        v_news.append((u_ref[:, sl] - jnp.dot(w_ref[:, sl], s_bf, preferred_element_type=F32)).astype(BF16))
        o_states.append(jnp.dot(qg_ref[:, sl], s_bf, preferred_element_type=F32))
    for hh in range(H_A):
        sl = slice(hh * DK_A, (hh + 1) * DK_A)
        s = s_ref[0, hh]
        v_bf = v_news[hh]
        o = o_states[hh] + jnp.dot(attn_ref[0, hh], v_bf, preferred_element_type=F32)
        decay_last = jnp.exp(gl_ref[0, 0, hh:hh + 1, :])
        s_ref[0, hh] = s * decay_last + lax.dot_general(kd_ref[:, sl], v_bf, (((0,), (0,)), ((), ())),
                                                         preferred_element_type=F32)
        o = o * lax.rsqrt(jnp.mean(o * o, axis=-1, keepdims=True) + RMS_EPS) * nw
        ya_ref[:, sl] = (o * _silu(z_ref[:, sl])).astype(ya_ref.dtype)


def _gdn_prompt(qkv, p_a, beta, gc, o_norm_w_l, batch, seq):
    hb = 8
    nhb = H_A // hb
    nc = seq // CHUNK
    w = hb * DK_A
    hw = H_A * DK_A
    n = batch * seq

    def per_head_cols(a):
        return a.reshape(batch, nc, CHUNK, nhb, hb).transpose(0, 3, 1, 2, 4)

    beta_c = per_head_cols(beta)
    gc_c = per_head_cols(gc)
    gc_r = gc_c.transpose(0, 1, 2, 4, 3)
    col_spec = pl.BlockSpec((1, 1, 1, CHUNK, hb), lambda b, h, c: (b, h, c, 0, 0))
    row_spec = pl.BlockSpec((1, 1, 1, hb, CHUNK), lambda b, h, c: (b, h, c, 0, 0))

    def tok_spec(col_block0):
        return pl.BlockSpec((CHUNK, w), lambda b, h, c: (b * nc + c, col_block0 + h))

    attn_spec = pl.BlockSpec((1, hb, CHUNK, CHUNK), lambda b, h, c: (b * nc + c, h, 0, 0))
    u, wf, qg, kd, attn = pl.pallas_call(
        functools.partial(_gdn_wy_body, hb=hb),
        grid=(batch, nhb, nc),
        in_specs=[tok_spec(0), tok_spec(hw // w), tok_spec(2 * hw // w), col_spec, col_spec, row_spec],
        out_specs=[tok_spec(0), tok_spec(0), tok_spec(0), tok_spec(0), attn_spec],
        out_shape=[jax.ShapeDtypeStruct((n, hw), F32), jax.ShapeDtypeStruct((n, hw), BF16),
                   jax.ShapeDtypeStruct((n, hw), BF16), jax.ShapeDtypeStruct((n, hw), BF16),
                   jax.ShapeDtypeStruct((batch * nc, H_A, CHUNK, CHUNK), BF16)],
        compiler_params=_cparams(("parallel", "parallel", "parallel")),
        name="gated_delta_wy",
    )(qkv, qkv, qkv, beta_c, gc_c, gc_r)

    g_last = jnp.broadcast_to(gc.reshape(batch, nc, CHUNK, H_A)[:, :, CHUNK - 1, :, None], (batch, nc, H_A, LANES))
    row = pl.BlockSpec((CHUNK, hw), lambda b, c: (b * nc + c, 0))
    ya, s_fin = pl.pallas_call(
        _gdn_scan_body,
        grid=(batch, nc),
        in_specs=[row, row, row, row,
                  pl.BlockSpec((1, H_A, CHUNK, CHUNK), lambda b, c: (b * nc + c, 0, 0, 0)),
                  pl.BlockSpec((1, 1, H_A, LANES), lambda b, c: (b, c, 0, 0)),
                  pl.BlockSpec((CHUNK, hw), lambda b, c: (b * nc + c, OFF_Z // hw)),
                  pl.BlockSpec((1, DV_A), lambda b, c: (0, 0))],
        out_specs=[row, pl.BlockSpec((1, H_A, DK_A, DV_A), lambda b, c: (b, 0, 0, 0))],
        out_shape=[jax.ShapeDtypeStruct((n, hw), BF16),
                   jax.ShapeDtypeStruct((batch, H_A, DK_A, DV_A), F32)],
        compiler_params=_cparams(("parallel", "arbitrary")),
        name="gated_delta_scan",
    )(u, wf, qg, kd, attn, g_last, p_a, o_norm_w_l.reshape(1, DV_A))
    return ya, s_fin


def _gdn_sample_body(pa_ref, cbuf_ref, cw_ref, bg_ref, s_ref, nw_ref, ya_ref, conv_ref, so_ref):
    new = pa_ref[0, :, :CONV_DIM]
    buf = cbuf_ref[0]
    w = cw_ref[...]
    acc = buf[0:1] * w[0:1]
    for t in range(1, CONV_W - 1):
        acc = acc + buf[t:t + 1] * w[t:t + 1]
    y = _silu(acc + new * w[CONV_W - 1:CONV_W])
    conv_ref[0, 0:2, :] = buf[1:3]
    conv_ref[0, 2:3, :] = new
    z = pa_ref[0, :, OFF_Z:OFF_BETA]
    bg = bg_ref[0]
    nw = nw_ref[...]
    ri = lax.broadcasted_iota(jnp.int32, (DK_A, DK_A), 0)
    ci = lax.broadcasted_iota(jnp.int32, (DK_A, DK_A), 1)
    eye = ri == ci

    def as_col(row):
        return jnp.sum(jnp.where(eye, row, 0.0), axis=1, keepdims=True)

    for h in range(H_A):
        qh = y[:, h * DK_A:(h + 1) * DK_A]
        kh = y[:, (H_A + h) * DK_A:(H_A + h + 1) * DK_A]
        vh = y[:, (2 * H_A + h) * DK_A:(2 * H_A + h + 1) * DK_A]
        qh = qh * lax.rsqrt(jnp.sum(qh * qh, axis=-1, keepdims=True) + RMS_EPS) * (DK_A ** -0.5)
        kh = kh * lax.rsqrt(jnp.sum(kh * kh, axis=-1, keepdims=True) + RMS_EPS)
        beta = bg[0:1, h:h + 1]
        g = bg[1:2, h:h + 1]
        k_col = as_col(kh)
        s = s_ref[0, h] * jnp.exp(g)
        kv = jnp.sum(_rb(s) * _rb(k_col), axis=0, keepdims=True)
        s = s + k_col * ((vh - kv) * beta)
        so_ref[0, h] = s
        o = jnp.sum(_rb(s) * _rb(as_col(qh)), axis=0, keepdims=True)
        o = o * lax.rsqrt(jnp.mean(o * o, axis=-1, keepdims=True) + RMS_EPS) * nw
        ya_ref[0, :, h * DV_A:(h + 1) * DV_A] = (o * _silu(z[:, h * DV_A:(h + 1) * DV_A])).astype(ya_ref.dtype)


def _gdn_sample(p_a_s, conv_buf, conv_w_l, beta_s, g_s, s_delta, o_norm_w_l):
    n = p_a_s.shape[0]
    bg = jnp.stack([beta_s, g_s], axis=1)
    return pl.pallas_call(
        _gdn_sample_body, grid=(n,),
        in_specs=[pl.BlockSpec((1, 1, OFF_BETA), lambda i: (i, 0, 0)),
                  pl.BlockSpec((1, CONV_W - 1, CONV_DIM), lambda i: (i, 0, 0)),
                  pl.BlockSpec((CONV_W, CONV_DIM), lambda i: (0, 0)),
                  pl.BlockSpec((1, 2, H_A), lambda i: (i, 0, 0)),
                  pl.BlockSpec((1, H_A, DK_A, DV_A), lambda i: (i, 0, 0, 0)),
                  pl.BlockSpec((1, DV_A), lambda i: (0, 0))],
        out_specs=[pl.BlockSpec((1, 1, H_A * DV_A), lambda i: (i, 0, 0)),
                   pl.BlockSpec((1, CONV_W - 1, CONV_DIM), lambda i: (i, 0, 0)),
                   pl.BlockSpec((1, H_A, DK_A, DV_A), lambda i: (i, 0, 0, 0))],
        out_shape=[jax.ShapeDtypeStruct((n, 1, H_A * DV_A), BF16),
                   jax.ShapeDtypeStruct((n, CONV_W - 1, CONV_DIM), F32),
                   jax.ShapeDtypeStruct((n, H_A, DK_A, DV_A), F32)],
        compiler_params=_cparams(("parallel",)),
        name="gated_delta_step",
    )(p_a_s, conv_buf, conv_w_l, bg, s_delta, o_norm_w_l.reshape(1, DV_A))


def _t5_bucket(dist):
    max_exact = NUM_BUCKETS // 2
    d = jnp.maximum(dist, 1).astype(F32)
    large = max_exact + (jnp.log(d / max_exact) / math.log(MAX_DISTANCE / max_exact)
                         * (NUM_BUCKETS - max_exact)).astype(jnp.int32)
    large = jnp.minimum(large, NUM_BUCKETS - 1)
    return jnp.where(dist < max_exact, dist, large)


def _attn_prompt_body(q_ref, kp_ref, kc_ref, vp_ref, vc_ref, bias_ref, o_ref, lse_ref, *, dil, hc):
    n = pl.program_id(1)
    hg = pl.program_id(2)
    kj = lax.broadcasted_iota(jnp.int32, (Q_BLOCK, 2 * Q_BLOCK), 1)
    in_seq = (n > 0) | (kj >= Q_BLOCK)
    units = [(pl.ds(r, Q_BLOCK, stride=dil) if dil > 1 else slice(None), h)
             for r in range(dil) for h in range(hc)]
    scores = []
    for rows, h in units:
        sl = slice(h * DH_B, (h + 1) * DH_B)
        kk = jnp.concatenate([kp_ref[rows, sl], kc_ref[rows, sl]], axis=0)
        scores.append(_dot_nt(q_ref[rows, sl], kk))
    for (rows, h), sc in zip(units, scores):
        sl = slice(h * DH_B, (h + 1) * DH_B)
        vv = jnp.concatenate([vp_ref[rows, sl], vc_ref[rows, sl]], axis=0)
        logits = jnp.where(in_seq, sc * (DH_B ** -0.5) + bias_ref[hg * hc + h], -jnp.inf)
        mx = jnp.max(logits, axis=-1, keepdims=True)
        e = jnp.exp(logits - mx)
        den = jnp.sum(e, axis=-1, keepdims=True)
        o_ref[rows, sl] = _dot(e / den, vv)
        lse_ref[rows, sl] = jnp.broadcast_to(mx + jnp.log(den), (Q_BLOCK, DH_B))


def _attn_prompt(p_b, gi, bias_tab, batch, seq):
    _, dil = GROUPS[gi]
    rb = Q_BLOCK * dil
    nb = seq // rb
    hc = H_B if dil == 1 else 1
    cw = hc * DH_B
    hw = H_B * DH_B

    def spec(which, prev):
        def imap(b, n, h):
            nn = jnp.maximum(n - 1, 0) if prev else n
            return (b * nb + nn, (gi * QKV_B + which * hw) // cw + h)
        return pl.BlockSpec((rb, cw), imap)

    out_spec = pl.BlockSpec((rb, cw), lambda b, n, h: (b * nb + n, h))
    shp = jax.ShapeDtypeStruct((batch * seq, hw), F32)
    return pl.pallas_call(
        functools.partial(_attn_prompt_body, dil=dil, hc=hc), grid=(batch, nb, H_B // hc),
        in_specs=[spec(0, False), spec(1, True), spec(1, False), spec(2, True), spec(2, False),
                  pl.BlockSpec((H_B, Q_BLOCK, 2 * Q_BLOCK), lambda b, n, h: (0, 0, 0))],
        out_specs=[out_spec, out_spec],
        out_shape=[shp, shp],
        compiler_params=_cparams(("parallel", "parallel", "parallel")),
        name=f"dilated_attn_prompt_g{gi}",
    )(p_b, p_b, p_b, p_b, p_b, bias_tab)


def _merge_body(o0, o1, o2, l0, l1, l2, yb_ref):
    ls = [l0[...], l1[...], l2[...]]
    mx = jnp.maximum(jnp.maximum(ls[0], ls[1]), ls[2])
    es = [jnp.exp(l - mx) for l in ls]
    den = es[0] + es[1] + es[2]
    acc = (_rb(es[0] / den) * _rb(o0[...]) + _rb(es[1] / den) * _rb(o1[...])
           + _rb(es[2] / den) * _rb(o2[...]))
    yb_ref[...] = acc.astype(yb_ref.dtype)


def _merge_groups(outs, lses):
    rows, hw = outs[0].shape
    tt = 512
    spec = pl.BlockSpec((tt, hw), lambda i: (i, 0))
    return pl.pallas_call(
        _merge_body, grid=(rows // tt,),
        in_specs=[spec] * 6, out_specs=spec,
        out_shape=jax.ShapeDtypeStruct((rows, hw), BF16),
        compiler_params=_cparams(("parallel",)),
        name="merge_attention_groups",
    )(*outs, *lses)


def _attn_sample_body(pb_ref, c0_ref, c1_ref, c2_ref, bc_ref, bn_ref, yb_ref):
    caches = (c0_ref, c1_ref, c2_ref)
    scale = DH_B ** -0.5
    outs, lses = [], []
    for gi in range(N_GROUPS):
        q = pb_ref[0, 3 * gi]
        k_new = pb_ref[0, 3 * gi + 1]
        v_new = pb_ref[0, 3 * gi + 2]
        kc = caches[gi][0, :, 0, 0]
        vc = caches[gi][0, :, 0, 1]
        qr = _rb(q)
        lc = jnp.sum(_rb(kc) * qr[None], axis=-1, keepdims=True) * scale + bc_ref[gi]
        ln = jnp.sum(_rb(k_new) * qr, axis=-1, keepdims=True) * scale + bn_ref[gi]
        mx = jnp.maximum(jnp.max(lc, axis=0), ln)
        ec = jnp.exp(lc - mx[None])
        en = jnp.exp(ln - mx)
        den = jnp.sum(ec, axis=0) + en
        outs.append(jnp.sum(_rb(ec / den[None]) * _rb(vc), axis=0) + _rb(en / den) * _rb(v_new))
        lses.append(mx + jnp.log(den))
    mx = jnp.maximum(jnp.maximum(lses[0], lses[1]), lses[2])
    es = [jnp.exp(l - mx) for l in lses]
    den = es[0] + es[1] + es[2]
    acc = _rb(es[0] / den) * _rb(outs[0]) + _rb(es[1] / den) * _rb(outs[1]) + _rb(es[2] / den) * _rb(outs[2])
    yb_ref[0] = acc.astype(yb_ref.dtype)


def _attn_sample(p_b_s, caches, bias_cache, bias_new):
    n = p_b_s.shape[0]
    views, specs = [], []
    for (win, dil), c in zip(GROUPS, caches):
        wb = c.shape[1]
        views.append(c.reshape(n, wb // dil, dil, 2, H_B, DH_B))
        specs.append(pl.BlockSpec((1, wb // dil, 1, 2, H_B, DH_B), lambda i: (i, 0, 0, 0, 0, 0)))
    return pl.pallas_call(
        _attn_sample_body, grid=(n,),
        in_specs=[pl.BlockSpec((1, 3 * N_GROUPS, H_B, DH_B), lambda i: (i, 0, 0, 0))] + specs
                 + [pl.BlockSpec((N_GROUPS, Q_BLOCK, H_B, 1), lambda i: (0, 0, 0, 0)),
                    pl.BlockSpec((N_GROUPS, H_B, 1), lambda i: (0, 0, 0))],
        out_specs=pl.BlockSpec((1, H_B, DH_B), lambda i: (i, 0, 0)),
        out_shape=jax.ShapeDtypeStruct((n, H_B, DH_B), BF16),
        compiler_params=_cparams(("parallel",)),
        name="dilated_attn_sample",
    )(p_b_s, *views, bias_cache, bias_new)


CACHE_SHIFT_ROWS = 512


def _cache_shift_body(cur_ref, nxt_ref, new_ref, o_ref):
    i = pl.program_id(1)
    t = cur_ref.shape[1]
    o_ref[0, 0:t - 1] = cur_ref[0, 1:t]

    @pl.when(i < pl.num_programs(1) - 1)
    def _():
        o_ref[0, t - 1:t] = nxt_ref[0]

    @pl.when(i == pl.num_programs(1) - 1)
    def _():
        o_ref[0, t - 1:t] = new_ref[0]


def _cache_shift(caches, new_rows):
    outs = []
    for g, (c, new) in enumerate(zip(caches, new_rows)):
        n, wb = c.shape[0], c.shape[1]
        t = min(wb, CACHE_SHIFT_ROWS)
        tail = c.shape[2:]
        zeros = (0,) * len(tail)
        outs.append(pl.pallas_call(
            _cache_shift_body, grid=(n, wb // t),
            in_specs=[pl.BlockSpec((1, t) + tail, lambda b, i: (b, i) + zeros),
                      pl.BlockSpec((1, 1) + tail, lambda b, i: (b, jnp.minimum((i + 1) * t, wb - 1)) + zeros),
                      pl.BlockSpec((1, 1) + tail, lambda b, i: (b, 0) + zeros)],
            out_specs=pl.BlockSpec((1, t) + tail, lambda b, i: (b, i) + zeros),
            out_shape=jax.ShapeDtypeStruct(c.shape, c.dtype),
            compiler_params=_cparams(("parallel", "arbitrary")),
            name=f"kv_cache_shift_g{g}",
        )(c, c, new))
    return outs


def _branch_body(ya_ref, yb_ref, wa_ref, wb_ref, ga_ref, gb_ref, o_ref):
    ba = jnp.dot(ya_ref[...], wa_ref[...].astype(BF16), preferred_element_type=F32)
    bb = jnp.dot(yb_ref[...], wb_ref[...].astype(BF16), preferred_element_type=F32)
    o_ref[...] = (ga_ref[...] * ba + gb_ref[...] * bb).astype(o_ref.dtype)


def _branch_merge(ya, yb, w_pa, w_pb, gates):
    rows = ya.shape[0]
    tm, tn = ROW_TILE, 512
    nj = D_MODEL // tn
    return pl.pallas_call(
        _branch_body, grid=(rows // tm, nj),
        in_specs=[pl.BlockSpec((tm, ya.shape[1]), lambda i, j: (i, 0)),
                  pl.BlockSpec((tm, yb.shape[1]), lambda i, j: (i, 0)),
                  pl.BlockSpec((w_pa.shape[0], tn), lambda i, j: (0, j)),
                  pl.BlockSpec((w_pb.shape[0], tn), lambda i, j: (0, j)),
                  pl.BlockSpec((tm, tn), lambda i, j: (i, j)),
                  pl.BlockSpec((tm, tn), lambda i, j: (i, nj + j))],
        out_specs=pl.BlockSpec((tm, tn), lambda i, j: (i, j)),
        out_shape=jax.ShapeDtypeStruct((rows, D_MODEL), BF16),
        compiler_params=_cparams(("parallel", "parallel")),
        name="branch_proj_gate",
    )(ya, yb, w_pa, w_pb, gates, gates)


def _ln_body(x_ref, r_ref, g_ref, b_ref, *out_refs, alpha):
    v = alpha * x_ref[...] + r_ref[...]
    mu = jnp.mean(v, axis=-1, keepdims=True)
    var = jnp.mean(jnp.square(v - mu), axis=-1, keepdims=True)
    y = (v - mu) * lax.rsqrt(var + LN_EPS) * g_ref[...] + b_ref[...]
    for o_ref in out_refs:
        o_ref[...] = y.astype(o_ref.dtype)


def _residual_ln(x, r, g, b, alpha, out_dtypes, name):
    rows, d = x.shape
    tt = 344
    spec = pl.BlockSpec((tt, d), lambda i: (i, 0))
    vec = pl.BlockSpec((1, d), lambda i: (0, 0))
    return pl.pallas_call(
        functools.partial(_ln_body, alpha=alpha), grid=(rows // tt,),
        in_specs=[spec, spec, vec, vec],
        out_specs=[spec] * len(out_dtypes),
        out_shape=[jax.ShapeDtypeStruct((rows, d), dt) for dt in out_dtypes],
        compiler_params=_cparams(("parallel",)),
        name=name,
    )(x, r, g.reshape(1, d), b.reshape(1, d))


def _router_body(h_ref, w_ref, b_ref, idx_ref, gate_ref):
    logits = _dot(h_ref[...], w_ref[...]) + b_ref[...]
    lane = lax.broadcasted_iota(jnp.int32, logits.shape, 1)
    vals, idxs = [], []
    for _ in range(TOP_K):
        mx = jnp.max(logits, axis=-1, keepdims=True)
        idx = jnp.min(jnp.where(logits == mx, lane, LANES), axis=-1, keepdims=True)
        vals.append(mx)
        idxs.append(idx)
        logits = jnp.where(lane == idx, -jnp.inf, logits)
    es = [jnp.exp(v - vals[0]) for v in vals]
    den = es[0] + es[1] + es[2] + es[3]
    idx_out = jnp.zeros(logits.shape, jnp.int32)
    gate_out = jnp.zeros(logits.shape, F32)
    for k in range(TOP_K):
        idx_out = jnp.where(lane == k, idxs[k], idx_out)
        gate_out = jnp.where(lane == k, es[k] / den, gate_out)
    idx_ref[...] = idx_out
    gate_ref[...] = gate_out


def _router(h, router_w_l, router_b_l):
    rows, d = h.shape
    tt = 344
    w = jnp.pad(router_w_l, ((0, 0), (0, LANES - N_EXPERTS)))
    b = jnp.concatenate([router_b_l, jnp.full((LANES - N_EXPERTS,), -jnp.inf, F32)]).reshape(1, LANES)
    spec = pl.BlockSpec((tt, LANES), lambda i: (i, 0))
    return pl.pallas_call(
        _router_body, grid=(rows // tt,),
        in_specs=[pl.BlockSpec((tt, d), lambda i: (i, 0)),
                  pl.BlockSpec((d, LANES), lambda i: (0, 0)),
                  pl.BlockSpec((1, LANES), lambda i: (0, 0))],
        out_specs=[spec, spec],
        out_shape=[jax.ShapeDtypeStruct((rows, LANES), jnp.int32),
                   jax.ShapeDtypeStruct((rows, LANES), F32)],
        compiler_params=_cparams(("parallel",)),
        name="router_top4",
    )(h, w, b)


def _gmm1_body(be_ref, first_ref, nsub_ref, x_ref, wg_ref, wl_ref, bg_ref, bl_ref, a_ref, wg_bf, wl_bf):
    i = pl.program_id(1)

    @pl.when(first_ref[i] == 1)
    def _():
        wg_bf[...] = wg_ref[0].astype(BF16)
        wl_bf[...] = wl_ref[0].astype(BF16)

    for r in range(MOE_BLK // MOE_SUB + 1):
        @pl.when(nsub_ref[i] == r)
        def _():
            used = r * MOE_SUB
            if used:
                x = x_ref[0:used, :]
                hg = jnp.dot(x, wg_bf[...], preferred_element_type=F32) + bg_ref[0]
                hl = jnp.dot(x, wl_bf[...], preferred_element_type=F32) + bl_ref[0]
                glu = jnp.minimum(hg, SWIGLU_LIMIT)
                lin = jnp.clip(hl, -SWIGLU_LIMIT, SWIGLU_LIMIT)
                a_ref[0:used, :] = (glu * jax.nn.sigmoid(SWIGLU_ALPHA * glu) * (lin + 1.0)).astype(a_ref.dtype)
            if used < MOE_BLK:
                a_ref[used:MOE_BLK, :] = jnp.zeros((MOE_BLK - used, a_ref.shape[1]), a_ref.dtype)


def _gmm1(xs, w_gu_l, b_gu_l, block_e, first, nsub):
    n_slots, d = xs.shape
    nb = n_slots // MOE_BLK
    tn = 512
    nj = D_FF // tn
    b3 = b_gu_l.reshape(N_EXPERTS, 1, 2 * D_FF)
    grid_spec = pltpu.PrefetchScalarGridSpec(
        num_scalar_prefetch=3, grid=(nj, nb),
        in_specs=[pl.BlockSpec((MOE_BLK, d), lambda j, i, be, fs, ns: (i, 0)),
                  pl.BlockSpec((1, d, tn), lambda j, i, be, fs, ns: (be[i], 0, j)),
                  pl.BlockSpec((1, d, tn), lambda j, i, be, fs, ns: (be[i], 0, nj + j)),
                  pl.BlockSpec((1, 1, tn), lambda j, i, be, fs, ns: (be[i], 0, j)),
                  pl.BlockSpec((1, 1, tn), lambda j, i, be, fs, ns: (be[i], 0, nj + j))],
        out_specs=pl.BlockSpec((MOE_BLK, tn), lambda j, i, be, fs, ns: (i, j)),
        scratch_shapes=[pltpu.VMEM((d, tn), BF16), pltpu.VMEM((d, tn), BF16)])
    return pl.pallas_call(
        _gmm1_body, grid_spec=grid_spec,
        out_shape=jax.ShapeDtypeStruct((n_slots, D_FF), BF16),
        compiler_params=_cparams(("arbitrary", "arbitrary")),
        name="moe_up_swiglu",
    )(block_e, first, nsub, xs, w_gu_l, w_gu_l, b3, b3)


def _gmm2_body(be_ref, first_ref, nsub_ref, a_ref, w_ref, b_ref, y_ref, w_bf):
    i = pl.program_id(1)

    @pl.when(first_ref[i] == 1)
    def _():
        w_bf[...] = w_ref[0].astype(BF16)

    for r in range(MOE_BLK // MOE_SUB + 1):
        @pl.when(nsub_ref[i] == r)
        def _():
            used = r * MOE_SUB
            if used:
                y_ref[0:used, :] = jnp.dot(a_ref[0:used, :], w_bf[...], preferred_element_type=F32) + b_ref[0]
            if used < MOE_BLK:
                y_ref[used:MOE_BLK, :] = jnp.zeros((MOE_BLK - used, y_ref.shape[1]), y_ref.dtype)


def _gmm2(a, w_dn_l, b_dn_l, block_e, first, nsub):
    n_slots, f = a.shape
    nb = n_slots // MOE_BLK
    tn = min(1024, D_MODEL)
    nj = D_MODEL // tn
    b3 = b_dn_l.reshape(N_EXPERTS, 1, D_MODEL)
    grid_spec = pltpu.PrefetchScalarGridSpec(
        num_scalar_prefetch=3, grid=(nj, nb),
        in_specs=[pl.BlockSpec((MOE_BLK, f), lambda j, i, be, fs, ns: (i, 0)),
                  pl.BlockSpec((1, f, tn), lambda j, i, be, fs, ns: (be[i], 0, j)),
                  pl.BlockSpec((1, 1, tn), lambda j, i, be, fs, ns: (be[i], 0, j))],
        out_specs=pl.BlockSpec((MOE_BLK, tn), lambda j, i, be, fs, ns: (i, j)),
        scratch_shapes=[pltpu.VMEM((f, tn), BF16)])
    return pl.pallas_call(
        _gmm2_body, grid_spec=grid_spec,
        out_shape=jax.ShapeDtypeStruct((n_slots, D_MODEL), F32),
        compiler_params=_cparams(("arbitrary", "arbitrary")),
        name="moe_down",
    )(block_e, first, nsub, a, w_dn_l, b3)


COMBINE_TILE = 192


def _combine_row_copy(y_hbm, ybuf, sem, slot, t, k):
    return pltpu.make_async_copy(y_hbm.at[pl.ds(slot, 1)], ybuf.at[k, pl.ds(t, 1)], sem.at[k])


def _combine_ln_body(dest_ref, y_hbm, gate_ref, h_ref, g_ref, b_ref, o_ref, ybuf, sem, *, alpha):
    tt = h_ref.shape[0]

    def start_rows(t, carry):
        for k in range(TOP_K):
            _combine_row_copy(y_hbm, ybuf, sem, dest_ref[0, 0, t * TOP_K + k], t, k).start()
        return carry

    lax.fori_loop(0, tt, start_rows, 0)

    def wait_rows(t, carry):
        for k in range(TOP_K):
            _combine_row_copy(y_hbm, ybuf, sem, 0, t, k).wait()
        return carry

    lax.fori_loop(0, tt, wait_rows, 0)
    gate = gate_ref[...]
    f = gate[:, 0:1] * ybuf[0]
    for k in range(1, TOP_K):
        f = f + gate[:, k:k + 1] * ybuf[k]
    v = alpha * h_ref[...] + f
    mu = jnp.mean(v, axis=-1, keepdims=True)
    var = jnp.mean(jnp.square(v - mu), axis=-1, keepdims=True)
    o_ref[...] = (v - mu) * lax.rsqrt(var + LN_EPS) * g_ref[...] + b_ref[...]


def _combine_ln(h32, y, dest, gate_pad, g, b, alpha):
    rows, d = h32.shape
    tt = COMBINE_TILE
    nt = rows // tt
    row = pl.BlockSpec((tt, d), lambda i: (i, 0))
    vec = pl.BlockSpec((1, d), lambda i: (0, 0))
    return pl.pallas_call(
        functools.partial(_combine_ln_body, alpha=alpha), grid=(nt,),
        in_specs=[pl.BlockSpec((1, 1, tt * TOP_K), lambda i: (i, 0, 0), memory_space=pltpu.SMEM),
                  pl.BlockSpec(memory_space=pl.ANY),
                  pl.BlockSpec((tt, LANES), lambda i: (i, 0)), row, vec, vec],
        out_specs=row,
        out_shape=jax.ShapeDtypeStruct((rows, d), F32),
        scratch_shapes=[pltpu.VMEM((TOP_K, tt, d), F32), pltpu.SemaphoreType.DMA((TOP_K,))],
        compiler_params=_cparams(("arbitrary",)),
        name="moe_combine_ln2",
    )(dest.reshape(nt, 1, tt * TOP_K), y, gate_pad, h32, g.reshape(1, d), b.reshape(1, d))


def _moe(h32, hbf, n_tok, router_w_l, router_b_l, w_gu_l, b_gu_l, w_dn_l, b_dn_l):
    rows = h32.shape[0]
    idx_pad, gate_pad = _router(h32, router_w_l, router_b_l)
    top_idx = idx_pad[:n_tok, :TOP_K]
    gate = gate_pad[:n_tok, :TOP_K]
    n_assign = n_tok * TOP_K
    n_blocks = -(-n_assign // MOE_BLK) + N_EXPERTS
    n_slots = n_blocks * MOE_BLK
    flat_e = top_idx.reshape(-1)
    onehot = (flat_e[:, None] == jnp.arange(N_EXPERTS, dtype=jnp.int32)[None, :]).astype(jnp.int32)
    counts = jnp.sum(onehot, axis=0)
    padded = (counts + MOE_BLK - 1) // MOE_BLK * MOE_BLK
    pad_end = jnp.cumsum(padded)
    pad_start = pad_end - padded
    dest = jnp.sum((jnp.cumsum(onehot, axis=0) - onehot + pad_start[None, :]) * onehot, axis=1)
    slot_tok = jnp.zeros((n_slots,), jnp.int32).at[dest].set(jnp.arange(n_assign, dtype=jnp.int32) // TOP_K)
    n_used = pad_end[-1] // MOE_BLK
    blk = jnp.minimum(jnp.arange(n_blocks, dtype=jnp.int32), n_used - 1)
    block_e = jnp.minimum(jnp.sum((pad_end[None, :] <= (blk * MOE_BLK)[:, None]).astype(jnp.int32), axis=1),
                          N_EXPERTS - 1)
    first = jnp.concatenate([jnp.ones((1,), jnp.int32), (block_e[1:] != block_e[:-1]).astype(jnp.int32)])
    seg_rows = jnp.clip(counts[block_e] - (blk * MOE_BLK - pad_start[block_e]), 0, MOE_BLK)
    nsub = jnp.where(jnp.arange(n_blocks) < n_used, (seg_rows + MOE_SUB - 1) // MOE_SUB, 0).astype(jnp.int32)
    xs = hbf[slot_tok]
    a = _gmm1(xs, w_gu_l, b_gu_l, block_e, first, nsub)
    y = _gmm2(a, w_dn_l, b_dn_l, block_e, first, nsub)
    dest_pad = jnp.pad(dest.reshape(n_tok, TOP_K), ((0, rows - n_tok), (0, 0)))
    row_is_token = (jnp.arange(rows) < n_tok)[:, None]
    return y, dest_pad, jnp.where(row_is_token, gate_pad, 0.0)


def _attention_bias_tables(rel_bias):
    qi = jnp.arange(Q_BLOCK)[:, None]
    kj = jnp.arange(2 * Q_BLOCK)[None, :]
    rel = qi + Q_BLOCK - kj
    prompt_tabs, cache_tabs, new_tabs = [], [], []
    buckets = jnp.arange(NUM_BUCKETS, dtype=jnp.int32)
    for gi, (win, dil) in enumerate(GROUPS):
        steps = win // dil
        bias_g = rel_bias[:, gi * H_B:(gi + 1) * H_B].astype(F32)
        hit = _t5_bucket(jnp.maximum(rel, 0) * dil)[None, :, :, None] == buckets
        tab = jnp.sum(jnp.where(hit, bias_g.T[:, None, None, :], 0.0), axis=-1)
        prompt_tabs.append(jnp.where(((rel >= 0) & (rel <= steps))[None], tab, -jnp.inf))
        bj = bias_g[_t5_bucket(jnp.arange(steps + 1) * dil)]
        cache_tabs.append(bj[:0:-1, :, None])
        new_tabs.append(bj[0, :, None])
    return prompt_tabs, jnp.stack(cache_tabs), jnp.stack(new_tabs)


def kernel(x_prompt, x_sample, cache_kv_w128, cache_kv_w512, cache_kv_w2048, state_conv, state_delta, rel_bias,
           w_in, b_gate, conv_w, a_log, dt_bias, o_norm_w, w_branch_a, w_branch_b, w_out,
           ln1_g, ln1_b, router_w, router_b, w_gu, b_gu, w_dn, b_dn, ln2_g, ln2_b):
    batch, seq, d = x_prompt.shape
    n_s = x_sample.shape[0]
    n_p = batch * seq
    n_tok = n_p + n_s
    rows = 8 * ROW_TILE
    depth = w_in.shape[0]
    alpha = (2.0 * depth) ** 0.25
    kv_caches = (cache_kv_w128, cache_kv_w512, cache_kv_w2048)
    hw = H_B * DH_B

    x = jnp.concatenate([x_prompt.reshape(n_p, d), x_sample.reshape(n_s, d),
                         jnp.zeros((rows - n_tok, d), F32)], axis=0)
    prompt_tabs, bias_cache, bias_new = _attention_bias_tables(rel_bias)

    kv_p = [[] for _ in GROUPS]
    kv_s = [[] for _ in GROUPS]
    conv_p, conv_s, delta_p, delta_s = [], [], [], []
    for l in range(depth):
        xb = x.astype(BF16)
        w_in_t = jnp.swapaxes(w_in, 1, 2)
        in_proj = functools.partial(_mm, xb, w_in_t, layer=l, tm=ROW_TILE, w_is_nk=True)
        p_a = in_proj(n_cols=OFF_BETA, tn=512, name="in_proj_deltanet")
        p_ba = in_proj(n_cols=LANES, col0=OFF_BETA, tn=LANES, name="in_proj_beta_alpha")
        p_b = in_proj(n_cols=N_GROUPS * QKV_B, col0=OFF_B, tn=512, name="in_proj_attention")
        gates = in_proj(n_cols=2 * D_MODEL, col0=OFF_GATE, tn=512,
                        bias=b_gate[l].reshape(1, 2 * D_MODEL), name="in_proj_gates")

        beta, g, gc = _beta_g(p_ba, a_log[l], dt_bias[l])
        qkv = _conv_qkv_prompt(p_a, conv_w[l], n_p, seq)
        ya_p, s_p = _gdn_prompt(qkv, p_a, beta[:n_p], gc[:n_p], o_norm_w[l], batch, seq)
        ya_s, cs, s_s = _gdn_sample(p_a[n_p:n_tok].reshape(n_s, 1, OFF_BETA), state_conv[l], conv_w[l],
                                    beta[n_p:n_tok], g[n_p:n_tok], state_delta[l], o_norm_w[l])
        ya = jnp.concatenate([ya_p, ya_s.reshape(n_s, H_A * DV_A),
                              jnp.zeros((rows - n_tok, H_A * DV_A), BF16)], axis=0)

        outs, lses = [], []
        for gi in range(N_GROUPS):
            o_g, lse_g = _attn_prompt(p_b, gi, prompt_tabs[gi], batch, seq)
            outs.append(o_g)
            lses.append(lse_g)
        yb_p = _merge_groups(outs, lses)
        p_b_s = p_b[n_p:n_tok]
        yb_s = _attn_sample(p_b_s.reshape(n_s, 3 * N_GROUPS, H_B, DH_B), [c[l] for c in kv_caches],
                            bias_cache, bias_new)
        yb = jnp.concatenate([yb_p, yb_s.reshape(n_s, hw), jnp.zeros((rows - n_tok, hw), BF16)], axis=0)

        merged = _branch_merge(ya, yb, w_branch_a[l], w_branch_b[l], gates)
        mix = _mm(merged, w_out, layer=l, n_cols=D_MODEL, tm=ROW_TILE, tn=512, name="out_proj")
        h32, hbf = _residual_ln(x, mix, ln1_g[l], ln1_b[l], alpha, (F32, BF16), "deepnorm_ln1")
        y_slots, dest, gate = _moe(h32, hbf, n_tok, router_w[l], router_b[l], w_gu[l], b_gu[l], w_dn[l], b_dn[l])
        x = _combine_ln(h32, y_slots, dest, gate, ln2_g[l], ln2_b[l], alpha)

        new_rows = []
        for gi, (win, dil) in enumerate(GROUPS):
            keep = min(win, seq)
            c0, c1 = gi * QKV_B + hw, (gi + 1) * QKV_B
            last = [p_b[(b + 1) * seq - keep:(b + 1) * seq, c0:c1] for b in range(batch)]
            kv_p[gi].append(jnp.stack(last).reshape(batch, keep, 2, H_B, DH_B))
            new_rows.append(p_b_s[:, c0:c1].reshape(n_s, 1, 2, H_B, DH_B))
        for gi, shifted in enumerate(_cache_shift([c[l] for c in kv_caches], new_rows)):
            kv_s[gi].append(shifted)
        conv_p.append(jnp.stack([p_a[(b + 1) * seq - (CONV_W - 1):(b + 1) * seq, :CONV_DIM] for b in range(batch)]))
        conv_s.append(cs)
        delta_p.append(s_p)
        delta_s.append(s_s)

    return (x[:n_p].reshape(batch, seq, d), x[n_p:n_tok].reshape(n_s, 1, d),
            jnp.stack(kv_p[0]), jnp.stack(kv_s[0]),
            jnp.stack(kv_p[1]), jnp.stack(kv_s[1]),
            jnp.stack(kv_p[2]), jnp.stack(kv_s[2]),
            jnp.stack(conv_p), jnp.stack(conv_s),
            jnp.stack(delta_p), jnp.stack(delta_s))
```

```python
import functools
import math

import jax
import jax.numpy as jnp
from jax import lax
from jax.experimental import pallas as pl
from jax.experimental.pallas import tpu as pltpu

F32 = jnp.float32
BF16 = jnp.bfloat16

D_MODEL = 4096
H_A, DK_A, DV_A = 16, 128, 128
CONV_W = 4
CONV_DIM = H_A * (2 * DK_A + DV_A)
CHUNK = 64
H_B, DH_B = 8, 128
GROUPS = ((128, 1), (512, 4), (2048, 16))
N_GROUPS = len(GROUPS)
Q_BLOCK = 128
NUM_BUCKETS, MAX_DISTANCE = 32, 2048
N_EXPERTS, TOP_K = 32, 4
D_FF = D_MODEL
SWIGLU_LIMIT, SWIGLU_ALPHA = 7.0, 1.702
LN_EPS, RMS_EPS = 1e-5, 1e-6

OFF_Z = CONV_DIM
OFF_BETA = OFF_Z + H_A * DV_A
OFF_ALPHA = OFF_BETA + H_A
OFF_B = OFF_ALPHA + H_A
QKV_B = 3 * H_B * DH_B
OFF_GATE = OFF_B + N_GROUPS * QKV_B

LANES = 128
VMEM_LIMIT = 56 * 1024 * 1024
ROW_TILE = 1032
MOE_BLK = 512
MOE_SUB = 128


def _cparams(sem):
    return pltpu.CompilerParams(dimension_semantics=sem, vmem_limit_bytes=VMEM_LIMIT)


def _dot(a, b):
    return jnp.dot(a.astype(BF16), b.astype(BF16), preferred_element_type=F32)


def _dot_nt(a, b):
    return lax.dot_general(a.astype(BF16), b.astype(BF16), (((1,), (1,)), ((), ())),
                           preferred_element_type=F32)


def _dot_tn(a, b):
    return lax.dot_general(a.astype(BF16), b.astype(BF16), (((0,), (0,)), ((), ())),
                           preferred_element_type=F32)


def _dot_hi(a, b):
    return jnp.dot(a, b, precision=lax.Precision.HIGHEST, preferred_element_type=F32)


def _silu(x):
    return x * jax.nn.sigmoid(x)


def _rb(x):
    return x.astype(BF16).astype(F32)


def _mm_body(*refs, sigmoid_bias, w_is_nk):
    if sigmoid_bias:
        x_ref, w_ref, b_ref, o_ref = refs
    else:
        x_ref, w_ref, o_ref = refs
    w = w_ref[0].astype(BF16)
    contract = (((1,), (1,)), ((), ())) if w_is_nk else (((1,), (0,)), ((), ()))
    acc = lax.dot_general(x_ref[...], w, contract, preferred_element_type=F32)
    if sigmoid_bias:
        acc = jax.nn.sigmoid(acc + b_ref[...])
    o_ref[...] = acc.astype(o_ref.dtype)


def _mm(x, w, *, layer=0, n_cols, col0=0, tm, tn, w_is_nk=False, bias=None, out_dtype=F32, name):
    m, k = x.shape
    if w_is_nk:
        w_spec = pl.BlockSpec((pl.Element(1), pl.Element(tn), pl.Element(k)),
                              lambda i, j: (layer, pl.multiple_of(col0 + j * tn, 8), 0))
    else:
        w_spec = pl.BlockSpec((1, k, tn), lambda i, j: (layer, 0, j + col0 // tn))
    in_specs = [pl.BlockSpec((tm, k), lambda i, j: (i, 0)), w_spec]
    args = [x, w]
    if bias is not None:
        in_specs.append(pl.BlockSpec((1, tn), lambda i, j: (0, j)))
        args.append(bias)
    return pl.pallas_call(
        functools.partial(_mm_body, sigmoid_bias=bias is not None, w_is_nk=w_is_nk),
        grid=(m // tm, n_cols // tn),
        in_specs=in_specs,
        out_specs=pl.BlockSpec((tm, tn), lambda i, j: (i, j)),
        out_shape=jax.ShapeDtypeStruct((m, n_cols), out_dtype),
        compiler_params=_cparams(("parallel", "parallel")),
        name=name,
    )(*args)


def _bg_body(p_ref, a_ref, dt_ref, beta_ref, g_ref, gc_ref):
    p = p_ref[...]
    tt = p.shape[0]
    beta_ref[...] = jax.nn.sigmoid(p)
    g = -jnp.exp(a_ref[...]) * jax.nn.softplus(p + dt_ref[...])
    g_ref[...] = g
    ri = lax.broadcasted_iota(jnp.int32, (tt, tt), 0)
    ci = lax.broadcasted_iota(jnp.int32, (tt, tt), 1)
    same_chunk_before = ((ri // CHUNK) == (ci // CHUNK)) & (ci <= ri)
    gc_ref[...] = _dot_hi(same_chunk_before.astype(F32), g)


def _beta_g(p_ba, a_log_l, dt_bias_l):
    rows = p_ba.shape[0]
    tt = 3 * CHUNK
    zeros = jnp.zeros((LANES - 2 * H_A,), F32)
    a_vec = jnp.concatenate([jnp.zeros((H_A,), F32), a_log_l, zeros]).reshape(1, LANES)
    dt_vec = jnp.concatenate([jnp.zeros((H_A,), F32), dt_bias_l, zeros]).reshape(1, LANES)
    row_spec = pl.BlockSpec((tt, LANES), lambda i: (i, 0))
    vec_spec = pl.BlockSpec((1, LANES), lambda i: (0, 0))
    shp = jax.ShapeDtypeStruct((rows, LANES), F32)
    beta, g, gc = pl.pallas_call(
        _bg_body, grid=(rows // tt,),
        in_specs=[row_spec, vec_spec, vec_spec],
        out_specs=[row_spec, row_spec, row_spec],
        out_shape=[shp, shp, shp],
        compiler_params=_cparams(("parallel",)),
        name="deltanet_gates",
    )(p_ba, a_vec, dt_vec)
    return beta[:, :H_A], g[:, H_A:2 * H_A], gc[:, H_A:2 * H_A]


def _conv_body(cur_ref, halo_ref, w_ref, o_ref, *, seq, tt):
    i = pl.program_id(0)
    j = pl.program_id(1)
    cur = cur_ref[...]
    halo = jnp.where((i * tt) % seq == 0, 0.0, halo_ref[...])
    xh = jnp.concatenate([halo, cur], axis=0)
    w = w_ref[...]
    acc = xh[5:5 + tt] * w[0:1]
    for t in range(1, CONV_W):
        acc = acc + xh[5 + t:5 + t + tt] * w[t:t + 1]
    y = _silu(acc)
    ct = y.shape[1]
    n_qk_blocks = 2 * H_A * DK_A // ct
    is_q = j < H_A * DK_A // ct
    is_qk = j < n_qk_blocks
    scale = jnp.where(is_q, DK_A ** -0.5, 1.0)
    for h in range(ct // DK_A):
        sl = slice(h * DK_A, (h + 1) * DK_A)
        yh = y[:, sl]
        nrm = yh * lax.rsqrt(jnp.sum(yh * yh, axis=-1, keepdims=True) + RMS_EPS) * scale
        o_ref[:, sl] = jnp.where(is_qk, nrm, yh)


def _conv_qkv_prompt(p_a, conv_w_l, n_rows, seq):
    tt, ct = 256, 1024
    return pl.pallas_call(
        functools.partial(_conv_body, seq=seq, tt=tt),
        grid=(n_rows // tt, CONV_DIM // ct),
        in_specs=[pl.BlockSpec((tt, ct), lambda i, j: (i, j)),
                  pl.BlockSpec((8, ct), lambda i, j: (jnp.maximum(i * (tt // 8) - 1, 0), j)),
                  pl.BlockSpec((CONV_W, ct), lambda i, j: (0, j))],
        out_specs=pl.BlockSpec((tt, ct), lambda i, j: (i, j)),
        out_shape=jax.ShapeDtypeStruct((n_rows, CONV_DIM), F32),
        compiler_params=_cparams(("parallel", "parallel")),
        name="conv_silu_l2norm",
    )(p_a, p_a, conv_w_l)


def _split_bf16(a):
    hi = a.astype(BF16)
    return hi, (a - hi.astype(F32)).astype(BF16)


def _dot_3pass(a, b):
    a_hi, a_lo = _split_bf16(a)
    b_hi, b_lo = _split_bf16(b)
    return (jnp.dot(a_hi, b_hi, preferred_element_type=F32) + jnp.dot(a_hi, b_lo, preferred_element_type=F32)
            + jnp.dot(a_lo, b_hi, preferred_element_type=F32))


def _gdn_wy_body(q_ref, k_ref, v_ref, b_ref, gc_ref, gr_ref, u_ref, w_ref, qg_ref, kd_ref, attn_ref, *, hb):
    C = CHUNK
    ri = lax.broadcasted_iota(jnp.int32, (C, C), 0)
    ci = lax.broadcasted_iota(jnp.int32, (C, C), 1)
    tril = ci <= ri
    strict = ci < ri
    ys, zs = [], []
    for hh in range(hb):
        sl = slice(hh * DK_A, (hh + 1) * DK_A)
        q = q_ref[:, sl]
        k = k_ref[:, sl]
        v = v_ref[:, sl]
        b = b_ref[0, 0, 0][:, hh:hh + 1]
        gc = gc_ref[0, 0, 0][:, hh:hh + 1]
        gr = gr_ref[0, 0, 0][hh:hh + 1, :]
        decay = jnp.where(tril, jnp.exp(jnp.where(tril, gc - gr, 0.0)), 0.0)
        kb = k * b
        eg = jnp.exp(gc)
        ys.append(-jnp.where(strict, _dot_nt(kb, k) * decay, 0.0))
        zs.append(jnp.concatenate([v * b, kb * eg], axis=-1))
        qg_ref[:, sl] = (q * eg).astype(qg_ref.dtype)
        kd_ref[:, sl] = (k * jnp.exp(gc[C - 1:C, :] - gc)).astype(kd_ref.dtype)
        attn_ref[0, hh] = jnp.where(tril, _dot_nt(q, k) * decay, 0.0).astype(attn_ref.dtype)
    for step in range(6):
        for hh in range(hb):
            y, z = ys[hh], zs[hh]
            if step < 5:
                prod = _dot_3pass(y, jnp.concatenate([z, y], axis=-1))
                zs[hh] = z + prod[:, :DV_A + DK_A]
                ys[hh] = prod[:, DV_A + DK_A:]
            else:
                zs[hh] = z + _dot_3pass(y, z)
    for hh in range(hb):
        sl = slice(hh * DK_A, (hh + 1) * DK_A)
        u_ref[:, sl] = zs[hh][:, :DV_A]
        w_ref[:, sl] = zs[hh][:, DV_A:].astype(w_ref.dtype)


def _gdn_scan_body(u_ref, w_ref, qg_ref, kd_ref, attn_ref, gl_ref, z_ref, nw_ref, ya_ref, s_ref):
    c = pl.program_id(1)

    @pl.when(c == 0)
    def _():
        s_ref[...] = jnp.zeros_like(s_ref)

    nw = nw_ref[...]
    v_news, o_states = [], []
    for hh in range(H_A):
        sl = slice(hh * DK_A, (hh + 1) * DK_A)
        s_bf = s_ref[0, hh].astype(BF16)
        v_news.append((u_ref[:, sl] - jnp.dot(w_ref[:, sl], s_bf, preferred_element_type=F32)).astype(BF16))
        o_states.append(jnp.dot(qg_ref[:, sl], s_bf, preferred_element_type=F32))
    for hh in range(H_A):
        sl = slice(hh * DK_A, (hh + 1) * DK_A)
        s = s_ref[0, hh]
        v_bf = v_news[hh]
        o = o_states[hh] + jnp.dot(attn_ref[0, hh], v_bf, preferred_element_type=F32)
        decay_last = jnp.exp(gl_ref[0, 0, hh:hh + 1, :])
        s_ref[0, hh] = s * decay_last + lax.dot_general(kd_ref[:, sl], v_bf, (((0,), (0,)), ((), ())),
                                                         preferred_element_type=F32)
        o = o * lax.rsqrt(jnp.mean(o * o, axis=-1, keepdims=True) + RMS_EPS) * nw
        ya_ref[:, sl] = (o * _silu(z_ref[:, sl])).astype(ya_ref.dtype)


def _gdn_prompt(qkv, p_a, beta, gc, o_norm_w_l, batch, seq):
    hb = 8
    nhb = H_A // hb
    nc = seq // CHUNK
    w = hb * DK_A
    hw = H_A * DK_A
    n = batch * seq

    def per_head_cols(a):
        return a.reshape(batch, nc, CHUNK, nhb, hb).transpose(0, 3, 1, 2, 4)

    beta_c = per_head_cols(beta)
    gc_c = per_head_cols(gc)
    gc_r = gc_c.transpose(0, 1, 2, 4, 3)
    col_spec = pl.BlockSpec((1, 1, 1, CHUNK, hb), lambda b, h, c: (b, h, c, 0, 0))
    row_spec = pl.BlockSpec((1, 1, 1, hb, CHUNK), lambda b, h, c: (b, h, c, 0, 0))

    def tok_spec(col_block0):
        return pl.BlockSpec((CHUNK, w), lambda b, h, c: (b * nc + c, col_block0 + h))

    attn_spec = pl.BlockSpec((1, hb, CHUNK, CHUNK), lambda b, h, c: (b * nc + c, h, 0, 0))
    u, wf, qg, kd, attn = pl.pallas_call(
        functools.partial(_gdn_wy_body, hb=hb),
        grid=(batch, nhb, nc),
        in_specs=[tok_spec(0), tok_spec(hw // w), tok_spec(2 * hw // w), col_spec, col_spec, row_spec],
        out_specs=[tok_spec(0), tok_spec(0), tok_spec(0), tok_spec(0), attn_spec],
        out_shape=[jax.ShapeDtypeStruct((n, hw), F32), jax.ShapeDtypeStruct((n, hw), BF16),
                   jax.ShapeDtypeStruct((n, hw), BF16), jax.ShapeDtypeStruct((n, hw), BF16),
                   jax.ShapeDtypeStruct((batch * nc, H_A, CHUNK, CHUNK), BF16)],
        compiler_params=_cparams(("parallel", "parallel", "parallel")),
        name="gated_delta_wy",
    )(qkv, qkv, qkv, beta_c, gc_c, gc_r)

    g_last = jnp.broadcast_to(gc.reshape(batch, nc, CHUNK, H_A)[:, :, CHUNK - 1, :, None], (batch, nc, H_A, LANES))
    row = pl.BlockSpec((CHUNK, hw), lambda b, c: (b * nc + c, 0))
    ya, s_fin = pl.pallas_call(
        _gdn_scan_body,
        grid=(batch, nc),
        in_specs=[row, row, row, row,
                  pl.BlockSpec((1, H_A, CHUNK, CHUNK), lambda b, c: (b * nc + c, 0, 0, 0)),
                  pl.BlockSpec((1, 1, H_A, LANES), lambda b, c: (b, c, 0, 0)),
                  pl.BlockSpec((CHUNK, hw), lambda b, c: (b * nc + c, OFF_Z // hw)),
                  pl.BlockSpec((1, DV_A), lambda b, c: (0, 0))],
        out_specs=[row, pl.BlockSpec((1, H_A, DK_A, DV_A), lambda b, c: (b, 0, 0, 0))],
        out_shape=[jax.ShapeDtypeStruct((n, hw), BF16),
                   jax.ShapeDtypeStruct((batch, H_A, DK_A, DV_A), F32)],
        compiler_params=_cparams(("parallel", "arbitrary")),
        name="gated_delta_scan",
    )(u, wf, qg, kd, attn, g_last, p_a, o_norm_w_l.reshape(1, DV_A))
    return ya, s_fin


def _gdn_sample_body(pa_ref, cbuf_ref, cw_ref, bg_ref, s_ref, nw_ref, ya_ref, conv_ref, so_ref):
    new = pa_ref[0, :, :CONV_DIM]
    buf = cbuf_ref[0]
    w = cw_ref[...]
    acc = buf[0:1] * w[0:1]
    for t in range(1, CONV_W - 1):
        acc = acc + buf[t:t + 1] * w[t:t + 1]
    y = _silu(acc + new * w[CONV_W - 1:CONV_W])
    conv_ref[0, 0:2, :] = buf[1:3]
    conv_ref[0, 2:3, :] = new
    z = pa_ref[0, :, OFF_Z:OFF_BETA]
    bg = bg_ref[0]
    nw = nw_ref[...]
    ri = lax.broadcasted_iota(jnp.int32, (DK_A, DK_A), 0)
    ci = lax.broadcasted_iota(jnp.int32, (DK_A, DK_A), 1)
    eye = ri == ci

    def as_col(row):
        return jnp.sum(jnp.where(eye, row, 0.0), axis=1, keepdims=True)

    for h in range(H_A):
        qh = y[:, h * DK_A:(h + 1) * DK_A]
        kh = y[:, (H_A + h) * DK_A:(H_A + h + 1) * DK_A]
        vh = y[:, (2 * H_A + h) * DK_A:(2 * H_A + h + 1) * DK_A]
        qh = qh * lax.rsqrt(jnp.sum(qh * qh, axis=-1, keepdims=True) + RMS_EPS) * (DK_A ** -0.5)
        kh = kh * lax.rsqrt(jnp.sum(kh * kh, axis=-1, keepdims=True) + RMS_EPS)
        beta = bg[0:1, h:h + 1]
        g = bg[1:2, h:h + 1]
        k_col = as_col(kh)
        s = s_ref[0, h] * jnp.exp(g)
        kv = jnp.sum(_rb(s) * _rb(k_col), axis=0, keepdims=True)
        s = s + k_col * ((vh - kv) * beta)
        so_ref[0, h] = s
        o = jnp.sum(_rb(s) * _rb(as_col(qh)), axis=0, keepdims=True)
        o = o * lax.rsqrt(jnp.mean(o * o, axis=-1, keepdims=True) + RMS_EPS) * nw
        ya_ref[0, :, h * DV_A:(h + 1) * DV_A] = (o * _silu(z[:, h * DV_A:(h + 1) * DV_A])).astype(ya_ref.dtype)


def _gdn_sample(p_a_s, conv_buf, conv_w_l, beta_s, g_s, s_delta, o_norm_w_l):
    n = p_a_s.shape[0]
    bg = jnp.stack([beta_s, g_s], axis=1)
    return pl.pallas_call(
        _gdn_sample_body, grid=(n,),
        in_specs=[pl.BlockSpec((1, 1, OFF_BETA), lambda i: (i, 0, 0)),
                  pl.BlockSpec((1, CONV_W - 1, CONV_DIM), lambda i: (i, 0, 0)),
                  pl.BlockSpec((CONV_W, CONV_DIM), lambda i: (0, 0)),
                  pl.BlockSpec((1, 2, H_A), lambda i: (i, 0, 0)),
                  pl.BlockSpec((1, H_A, DK_A, DV_A), lambda i: (i, 0, 0, 0)),
                  pl.BlockSpec((1, DV_A), lambda i: (0, 0))],
        out_specs=[pl.BlockSpec((1, 1, H_A * DV_A), lambda i: (i, 0, 0)),
                   pl.BlockSpec((1, CONV_W - 1, CONV_DIM), lambda i: (i, 0, 0)),
                   pl.BlockSpec((1, H_A, DK_A, DV_A), lambda i: (i, 0, 0, 0))],
        out_shape=[jax.ShapeDtypeStruct((n, 1, H_A * DV_A), BF16),
                   jax.ShapeDtypeStruct((n, CONV_W - 1, CONV_DIM), F32),
                   jax.ShapeDtypeStruct((n, H_A, DK_A, DV_A), F32)],
        compiler_params=_cparams(("parallel",)),
        name="gated_delta_step",
    )(p_a_s, conv_buf, conv_w_l, bg, s_delta, o_norm_w_l.reshape(1, DV_A))


def _t5_bucket(dist):
    max_exact = NUM_BUCKETS // 2
    d = jnp.maximum(dist, 1).astype(F32)
    large = max_exact + (jnp.log(d / max_exact) / math.log(MAX_DISTANCE / max_exact)
                         * (NUM_BUCKETS - max_exact)).astype(jnp.int32)
    large = jnp.minimum(large, NUM_BUCKETS - 1)
    return jnp.where(dist < max_exact, dist, large)


def _attn_prompt_body(q_ref, kp_ref, kc_ref, vp_ref, vc_ref, bias_ref, o_ref, lse_ref, *, dil, hc):
    n = pl.program_id(1)
    hg = pl.program_id(2)
    kj = lax.broadcasted_iota(jnp.int32, (Q_BLOCK, 2 * Q_BLOCK), 1)
    in_seq = (n > 0) | (kj >= Q_BLOCK)
    units = [(pl.ds(r, Q_BLOCK, stride=dil) if dil > 1 else slice(None), h)
             for r in range(dil) for h in range(hc)]
    scores = []
    for rows, h in units:
        sl = slice(h * DH_B, (h + 1) * DH_B)
        kk = jnp.concatenate([kp_ref[rows, sl], kc_ref[rows, sl]], axis=0)
        scores.append(_dot_nt(q_ref[rows, sl], kk))
    for (rows, h), sc in zip(units, scores):
        sl = slice(h * DH_B, (h + 1) * DH_B)
        vv = jnp.concatenate([vp_ref[rows, sl], vc_ref[rows, sl]], axis=0)
        logits = jnp.where(in_seq, sc * (DH_B ** -0.5) + bias_ref[hg * hc + h], -jnp.inf)
        mx = jnp.max(logits, axis=-1, keepdims=True)
        e = jnp.exp(logits - mx)
        den = jnp.sum(e, axis=-1, keepdims=True)
        o_ref[rows, sl] = _dot(e / den, vv)
        lse_ref[rows, sl] = jnp.broadcast_to(mx + jnp.log(den), (Q_BLOCK, DH_B))


def _attn_prompt(p_b, gi, bias_tab, batch, seq):
    _, dil = GROUPS[gi]
    rb = Q_BLOCK * dil
    nb = seq // rb
    hc = H_B if dil == 1 else 1
    cw = hc * DH_B
    hw = H_B * DH_B

    def spec(which, prev):
        def imap(b, n, h):
            nn = jnp.maximum(n - 1, 0) if prev else n
            return (b * nb + nn, (gi * QKV_B + which * hw) // cw + h)
        return pl.BlockSpec((rb, cw), imap)

    out_spec = pl.BlockSpec((rb, cw), lambda b, n, h: (b * nb + n, h))
    shp = jax.ShapeDtypeStruct((batch * seq, hw), F32)
    return pl.pallas_call(
        functools.partial(_attn_prompt_body, dil=dil, hc=hc), grid=(batch, nb, H_B // hc),
        in_specs=[spec(0, False), spec(1, True), spec(1, False), spec(2, True), spec(2, False),
                  pl.BlockSpec((H_B, Q_BLOCK, 2 * Q_BLOCK), lambda b, n, h: (0, 0, 0))],
        out_specs=[out_spec, out_spec],
        out_shape=[shp, shp],
        compiler_params=_cparams(("parallel", "parallel", "parallel")),
        name=f"dilated_attn_prompt_g{gi}",
    )(p_b, p_b, p_b, p_b, p_b, bias_tab)


def _merge_body(o0, o1, o2, l0, l1, l2, yb_ref):
    ls = [l0[...], l1[...], l2[...]]
    mx = jnp.maximum(jnp.maximum(ls[0], ls[1]), ls[2])
    es = [jnp.exp(l - mx) for l in ls]
    den = es[0] + es[1] + es[2]
    acc = (_rb(es[0] / den) * _rb(o0[...]) + _rb(es[1] / den) * _rb(o1[...])
           + _rb(es[2] / den) * _rb(o2[...]))
    yb_ref[...] = acc.astype(yb_ref.dtype)


def _merge_groups(outs, lses):
    rows, hw = outs[0].shape
    tt = 512
    spec = pl.BlockSpec((tt, hw), lambda i: (i, 0))
    return pl.pallas_call(
        _merge_body, grid=(rows // tt,),
        in_specs=[spec] * 6, out_specs=spec,
        out_shape=jax.ShapeDtypeStruct((rows, hw), BF16),
        compiler_params=_cparams(("parallel",)),
        name="merge_attention_groups",
    )(*outs, *lses)


def _attn_sample_body(pb_ref, c0_ref, c1_ref, c2_ref, bc_ref, bn_ref, yb_ref):
    caches = (c0_ref, c1_ref, c2_ref)
    scale = DH_B ** -0.5
    outs, lses = [], []
    for gi in range(N_GROUPS):
        q = pb_ref[0, 3 * gi]
        k_new = pb_ref[0, 3 * gi + 1]
        v_new = pb_ref[0, 3 * gi + 2]
        kc = caches[gi][0, :, 0, 0]
        vc = caches[gi][0, :, 0, 1]
        qr = _rb(q)
        lc = jnp.sum(_rb(kc) * qr[None], axis=-1, keepdims=True) * scale + bc_ref[gi]
        ln = jnp.sum(_rb(k_new) * qr, axis=-1, keepdims=True) * scale + bn_ref[gi]
        mx = jnp.maximum(jnp.max(lc, axis=0), ln)
        ec = jnp.exp(lc - mx[None])
        en = jnp.exp(ln - mx)
        den = jnp.sum(ec, axis=0) + en
        outs.append(jnp.sum(_rb(ec / den[None]) * _rb(vc), axis=0) + _rb(en / den) * _rb(v_new))
        lses.append(mx + jnp.log(den))
    mx = jnp.maximum(jnp.maximum(lses[0], lses[1]), lses[2])
    es = [jnp.exp(l - mx) for l in lses]
    den = es[0] + es[1] + es[2]
    acc = _rb(es[0] / den) * _rb(outs[0]) + _rb(es[1] / den) * _rb(outs[1]) + _rb(es[2] / den) * _rb(outs[2])
    yb_ref[0] = acc.astype(yb_ref.dtype)


def _attn_sample(p_b_s, caches, bias_cache, bias_new):
    n = p_b_s.shape[0]
    views, specs = [], []
    for (win, dil), c in zip(GROUPS, caches):
        wb = c.shape[1]
        views.append(c.reshape(n, wb // dil, dil, 2, H_B, DH_B))
        specs.append(pl.BlockSpec((1, wb // dil, 1, 2, H_B, DH_B), lambda i: (i, 0, 0, 0, 0, 0)))
    return pl.pallas_call(
        _attn_sample_body, grid=(n,),
        in_specs=[pl.BlockSpec((1, 3 * N_GROUPS, H_B, DH_B), lambda i: (i, 0, 0, 0))] + specs
                 + [pl.BlockSpec((N_GROUPS, Q_BLOCK, H_B, 1), lambda i: (0, 0, 0, 0)),
                    pl.BlockSpec((N_GROUPS, H_B, 1), lambda i: (0, 0, 0))],
        out_specs=pl.BlockSpec((1, H_B, DH_B), lambda i: (i, 0, 0)),
        out_shape=jax.ShapeDtypeStruct((n, H_B, DH_B), BF16),
        compiler_params=_cparams(("parallel",)),
        name="dilated_attn_sample",
    )(p_b_s, *views, bias_cache, bias_new)


CACHE_SHIFT_ROWS = 512


def _cache_shift_body(cur_ref, nxt_ref, new_ref, o_ref):
    i = pl.program_id(1)
    t = cur_ref.shape[1]
    o_ref[0, 0:t - 1] = cur_ref[0, 1:t]

    @pl.when(i < pl.num_programs(1) - 1)
    def _():
        o_ref[0, t - 1:t] = nxt_ref[0]

    @pl.when(i == pl.num_programs(1) - 1)
    def _():
        o_ref[0, t - 1:t] = new_ref[0]


def _cache_shift(caches, new_rows):
    outs = []
    for g, (c, new) in enumerate(zip(caches, new_rows)):
        n, wb = c.shape[0], c.shape[1]
        t = min(wb, CACHE_SHIFT_ROWS)
        tail = c.shape[2:]
        zeros = (0,) * len(tail)
        outs.append(pl.pallas_call(
            _cache_shift_body, grid=(n, wb // t),
            in_specs=[pl.BlockSpec((1, t) + tail, lambda b, i: (b, i) + zeros),
                      pl.BlockSpec((1, 1) + tail, lambda b, i: (b, jnp.minimum((i + 1) * t, wb - 1)) + zeros),
                      pl.BlockSpec((1, 1) + tail, lambda b, i: (b, 0) + zeros)],
            out_specs=pl.BlockSpec((1, t) + tail, lambda b, i: (b, i) + zeros),
            out_shape=jax.ShapeDtypeStruct(c.shape, c.dtype),
            compiler_params=_cparams(("parallel", "arbitrary")),
            name=f"kv_cache_shift_g{g}",
        )(c, c, new))
    return outs


def _branch_body(ya_ref, yb_ref, wa_ref, wb_ref, ga_ref, gb_ref, o_ref):
    ba = jnp.dot(ya_ref[...], wa_ref[...].astype(BF16), preferred_element_type=F32)
    bb = jnp.dot(yb_ref[...], wb_ref[...].astype(BF16), preferred_element_type=F32)
    o_ref[...] = (ga_ref[...] * ba + gb_ref[...] * bb).astype(o_ref.dtype)


def _branch_merge(ya, yb, w_pa, w_pb, gates):
    rows = ya.shape[0]
    tm, tn = ROW_TILE, 512
    nj = D_MODEL // tn
    return pl.pallas_call(
        _branch_body, grid=(rows // tm, nj),
        in_specs=[pl.BlockSpec((tm, ya.shape[1]), lambda i, j: (i, 0)),
                  pl.BlockSpec((tm, yb.shape[1]), lambda i, j: (i, 0)),
                  pl.BlockSpec((w_pa.shape[0], tn), lambda i, j: (0, j)),
                  pl.BlockSpec((w_pb.shape[0], tn), lambda i, j: (0, j)),
                  pl.BlockSpec((tm, tn), lambda i, j: (i, j)),
                  pl.BlockSpec((tm, tn), lambda i, j: (i, nj + j))],
        out_specs=pl.BlockSpec((tm, tn), lambda i, j: (i, j)),
        out_shape=jax.ShapeDtypeStruct((rows, D_MODEL), BF16),
        compiler_params=_cparams(("parallel", "parallel")),
        name="branch_proj_gate",
    )(ya, yb, w_pa, w_pb, gates, gates)


def _ln_body(x_ref, r_ref, g_ref, b_ref, *out_refs, alpha):
    v = alpha * x_ref[...] + r_ref[...]
    mu = jnp.mean(v, axis=-1, keepdims=True)
    var = jnp.mean(jnp.square(v - mu), axis=-1, keepdims=True)
    y = (v - mu) * lax.rsqrt(var + LN_EPS) * g_ref[...] + b_ref[...]
    for o_ref in out_refs:
        o_ref[...] = y.astype(o_ref.dtype)


def _residual_ln(x, r, g, b, alpha, out_dtypes, name):
    rows, d = x.shape
    tt = 344
    spec = pl.BlockSpec((tt, d), lambda i: (i, 0))
    vec = pl.BlockSpec((1, d), lambda i: (0, 0))
    return pl.pallas_call(
        functools.partial(_ln_body, alpha=alpha), grid=(rows // tt,),
        in_specs=[spec, spec, vec, vec],
        out_specs=[spec] * len(out_dtypes),
        out_shape=[jax.ShapeDtypeStruct((rows, d), dt) for dt in out_dtypes],
        compiler_params=_cparams(("parallel",)),
        name=name,
    )(x, r, g.reshape(1, d), b.reshape(1, d))


def _router_body(h_ref, w_ref, b_ref, idx_ref, gate_ref):
    logits = _dot(h_ref[...], w_ref[...]) + b_ref[...]
    lane = lax.broadcasted_iota(jnp.int32, logits.shape, 1)
    vals, idxs = [], []
    for _ in range(TOP_K):
        mx = jnp.max(logits, axis=-1, keepdims=True)
        idx = jnp.min(jnp.where(logits == mx, lane, LANES), axis=-1, keepdims=True)
        vals.append(mx)
        idxs.append(idx)
        logits = jnp.where(lane == idx, -jnp.inf, logits)
    es = [jnp.exp(v - vals[0]) for v in vals]
    den = es[0] + es[1] + es[2] + es[3]
    idx_out = jnp.zeros(logits.shape, jnp.int32)
    gate_out = jnp.zeros(logits.shape, F32)
    for k in range(TOP_K):
        idx_out = jnp.where(lane == k, idxs[k], idx_out)
        gate_out = jnp.where(lane == k, es[k] / den, gate_out)
    idx_ref[...] = idx_out
    gate_ref[...] = gate_out


def _router(h, router_w_l, router_b_l):
    rows, d = h.shape
    tt = 344
    w = jnp.pad(router_w_l, ((0, 0), (0, LANES - N_EXPERTS)))
    b = jnp.concatenate([router_b_l, jnp.full((LANES - N_EXPERTS,), -jnp.inf, F32)]).reshape(1, LANES)
    spec = pl.BlockSpec((tt, LANES), lambda i: (i, 0))
    return pl.pallas_call(
        _router_body, grid=(rows // tt,),
        in_specs=[pl.BlockSpec((tt, d), lambda i: (i, 0)),
                  pl.BlockSpec((d, LANES), lambda i: (0, 0)),
                  pl.BlockSpec((1, LANES), lambda i: (0, 0))],
        out_specs=[spec, spec],
        out_shape=[jax.ShapeDtypeStruct((rows, LANES), jnp.int32),
                   jax.ShapeDtypeStruct((rows, LANES), F32)],
        compiler_params=_cparams(("parallel",)),
        name="router_top4",
    )(h, w, b)


def _expert_weight_copy(w_hbm, wbuf, sem, expert, col, slot, part):
    tn = wbuf.shape[-1]
    return pltpu.make_async_copy(w_hbm.at[expert, :, pl.ds(col, tn)], wbuf.at[slot, part], sem.at[slot, part])


def _expert_weights_step(sched_ref, nseg_ref, w_hbm, wbuf, w_bf, sem, col_of):
    j, i = pl.program_id(0), pl.program_id(1)
    nj = pl.num_programs(0)
    nseg = nseg_ref[0]
    n_parts = wbuf.shape[1]

    @pl.when(sched_ref[1, i] == 1)
    def _():
        seg = sched_ref[3, i]
        slot = (j * nseg + seg) % 2
        is_last_seg = seg == nseg - 1

        @pl.when((j == 0) & (i == 0))
        def _():
            for part in range(n_parts):
                _expert_weight_copy(w_hbm, wbuf, sem, sched_ref[0, 0], col_of(0, part), 0, part).start()

        for part in range(n_parts):
            _expert_weight_copy(w_hbm, wbuf, sem, 0, 0, slot, part).wait()

        @pl.when(jnp.logical_not(is_last_seg & (j == nj - 1)))
        def _():
            nxt_j = jnp.where(is_last_seg, j + 1, j)
            for part in range(n_parts):
                _expert_weight_copy(w_hbm, wbuf, sem, sched_ref[4, i], col_of(nxt_j, part), 1 - slot, part).start()

        for part in range(n_parts):
            w_bf[part] = wbuf[slot, part].astype(BF16)


def _gmm1_body(sched_ref, nseg_ref, x_ref, w_hbm, bg_ref, bl_ref, a_ref, wbuf, w_bf, sem):
    i = pl.program_id(1)
    tn = a_ref.shape[1]
    _expert_weights_step(sched_ref, nseg_ref, w_hbm, wbuf, w_bf, sem,
                         lambda j, part: pl.multiple_of(part * D_FF + j * tn, LANES))
    for r in range(MOE_BLK // MOE_SUB + 1):
        @pl.when(sched_ref[2, i] == r)
        def _():
            used = r * MOE_SUB
            if used:
                x = x_ref[0:used, :]
                hg = jnp.dot(x, w_bf[0], preferred_element_type=F32) + bg_ref[0]
                hl = jnp.dot(x, w_bf[1], preferred_element_type=F32) + bl_ref[0]
                glu = jnp.minimum(hg, SWIGLU_LIMIT)
                lin = jnp.clip(hl, -SWIGLU_LIMIT, SWIGLU_LIMIT)
                a_ref[0:used, :] = (glu * jax.nn.sigmoid(SWIGLU_ALPHA * glu) * (lin + 1.0)).astype(a_ref.dtype)
            if used < MOE_BLK:
                a_ref[used:MOE_BLK, :] = jnp.zeros((MOE_BLK - used, a_ref.shape[1]), a_ref.dtype)


def _gmm_scratch(k, tn, n_parts):
    return [pltpu.VMEM((2, n_parts, k, tn), F32), pltpu.VMEM((n_parts, k, tn), BF16),
            pltpu.SemaphoreType.DMA((2, n_parts))]


def _gmm1(xs, w_gu_l, b_gu_l, sched, nseg):
    n_slots, d = xs.shape
    nb = n_slots // MOE_BLK
    tn = min(512, D_FF)
    nj = D_FF // tn
    b3 = b_gu_l.reshape(N_EXPERTS, 1, 2 * D_FF)
    grid_spec = pltpu.PrefetchScalarGridSpec(
        num_scalar_prefetch=2, grid=(nj, nb),
        in_specs=[pl.BlockSpec((MOE_BLK, d), lambda j, i, sc, ns: (i, 0)),
                  pl.BlockSpec(memory_space=pl.ANY),
                  pl.BlockSpec((1, 1, tn), lambda j, i, sc, ns: (sc[0, i], 0, j)),
                  pl.BlockSpec((1, 1, tn), lambda j, i, sc, ns: (sc[0, i], 0, nj + j))],
        out_specs=pl.BlockSpec((MOE_BLK, tn), lambda j, i, sc, ns: (i, j)),
        scratch_shapes=_gmm_scratch(d, tn, 2))
    return pl.pallas_call(
        _gmm1_body, grid_spec=grid_spec,
        out_shape=jax.ShapeDtypeStruct((n_slots, D_FF), BF16),
        compiler_params=_cparams(("arbitrary", "arbitrary")),
        name="moe_up_swiglu",
    )(sched, nseg, xs, w_gu_l, b3, b3)


def _gmm2_body(sched_ref, nseg_ref, a_ref, w_hbm, b_ref, y_ref, wbuf, w_bf, sem):
    i = pl.program_id(1)
    tn = y_ref.shape[1]
    _expert_weights_step(sched_ref, nseg_ref, w_hbm, wbuf, w_bf, sem,
                         lambda j, part: pl.multiple_of(j * tn, LANES))
    for r in range(MOE_BLK // MOE_SUB + 1):
        @pl.when(sched_ref[2, i] == r)
        def _():
            used = r * MOE_SUB
            if used:
                y_ref[0:used, :] = jnp.dot(a_ref[0:used, :], w_bf[0], preferred_element_type=F32) + b_ref[0]
            if used < MOE_BLK:
                y_ref[used:MOE_BLK, :] = jnp.zeros((MOE_BLK - used, y_ref.shape[1]), y_ref.dtype)


def _gmm2(a, w_dn_l, b_dn_l, sched, nseg):
    n_slots, f = a.shape
    nb = n_slots // MOE_BLK
    tn = min(1024, D_MODEL)
    nj = D_MODEL // tn
    b3 = b_dn_l.reshape(N_EXPERTS, 1, D_MODEL)
    grid_spec = pltpu.PrefetchScalarGridSpec(
        num_scalar_prefetch=2, grid=(nj, nb),
        in_specs=[pl.BlockSpec((MOE_BLK, f), lambda j, i, sc, ns: (i, 0)),
                  pl.BlockSpec(memory_space=pl.ANY),
                  pl.BlockSpec((1, 1, tn), lambda j, i, sc, ns: (sc[0, i], 0, j))],
        out_specs=pl.BlockSpec((MOE_BLK, tn), lambda j, i, sc, ns: (i, j)),
        scratch_shapes=_gmm_scratch(f, tn, 1))
    return pl.pallas_call(
        _gmm2_body, grid_spec=grid_spec,
        out_shape=jax.ShapeDtypeStruct((n_slots, D_MODEL), F32),
        compiler_params=_cparams(("arbitrary", "arbitrary")),
        name="moe_down",
    )(sched, nseg, a, w_dn_l, b3)


def _gather_row_copy(h_hbm, buf, sem, tok, r):
    return pltpu.make_async_copy(h_hbm.at[pl.ds(tok, 1)], buf.at[pl.ds(r, 1)], sem.at[0])


def _moe_gather_body(nsub_ref, tok_ref, h_hbm, o_ref, buf, sem):
    used = nsub_ref[pl.program_id(0)] * MOE_SUB

    def start_row(r, carry):
        _gather_row_copy(h_hbm, buf, sem, tok_ref[0, 0, r], r).start()
        return carry

    lax.fori_loop(0, used, start_row, 0)

    def wait_row(r, carry):
        _gather_row_copy(h_hbm, buf, sem, 0, r).wait()
        return carry

    lax.fori_loop(0, used, wait_row, 0)
    for s in range(MOE_BLK // MOE_SUB):
        rows = slice(s * MOE_SUB, (s + 1) * MOE_SUB)

        @pl.when(s * MOE_SUB < used)
        def _():
            o_ref[rows, :] = buf[rows, :].astype(o_ref.dtype)

        @pl.when(s * MOE_SUB >= used)
        def _():
            o_ref[rows, :] = jnp.zeros((MOE_SUB, o_ref.shape[1]), o_ref.dtype)


def _moe_gather(h32, slot_tok, nsub):
    d = h32.shape[1]
    nb = nsub.shape[0]
    grid_spec = pltpu.PrefetchScalarGridSpec(
        num_scalar_prefetch=1, grid=(nb,),
        in_specs=[pl.BlockSpec((1, 1, MOE_BLK), lambda i, ns: (i, 0, 0), memory_space=pltpu.SMEM),
                  pl.BlockSpec(memory_space=pl.ANY)],
        out_specs=pl.BlockSpec((MOE_BLK, d), lambda i, ns: (i, 0)),
        scratch_shapes=[pltpu.VMEM((MOE_BLK, d), F32), pltpu.SemaphoreType.DMA((1,))])
    return pl.pallas_call(
        _moe_gather_body, grid_spec=grid_spec,
        out_shape=jax.ShapeDtypeStruct((nb * MOE_BLK, d), BF16),
        compiler_params=_cparams(("arbitrary",)),
        name="moe_gather_rows",
    )(nsub, slot_tok.reshape(nb, 1, MOE_BLK), h32)


COMBINE_TILE = 192


def _combine_row_copy(y_hbm, ybuf, sem, slot, t, k):
    return pltpu.make_async_copy(y_hbm.at[pl.ds(slot, 1)], ybuf.at[k, pl.ds(t, 1)], sem.at[k])


def _combine_ln_body(dest_ref, y_hbm, gate_ref, h_ref, g_ref, b_ref, o_ref, ybuf, sem, *, alpha):
    tt = h_ref.shape[0]

    def start_rows(t, carry):
        for k in range(TOP_K):
            _combine_row_copy(y_hbm, ybuf, sem, dest_ref[0, 0, t * TOP_K + k], t, k).start()
        return carry

    lax.fori_loop(0, tt, start_rows, 0)

    def wait_rows(t, carry):
        for k in range(TOP_K):
            _combine_row_copy(y_hbm, ybuf, sem, 0, t, k).wait()
        return carry

    lax.fori_loop(0, tt, wait_rows, 0)
    gate = gate_ref[...]
    f = gate[:, 0:1] * ybuf[0]
    for k in range(1, TOP_K):
        f = f + gate[:, k:k + 1] * ybuf[k]
    v = alpha * h_ref[...] + f
    mu = jnp.mean(v, axis=-1, keepdims=True)
    var = jnp.mean(jnp.square(v - mu), axis=-1, keepdims=True)
    o_ref[...] = (v - mu) * lax.rsqrt(var + LN_EPS) * g_ref[...] + b_ref[...]


def _combine_ln(h32, y, dest, gate_pad, g, b, alpha):
    rows, d = h32.shape
    tt = COMBINE_TILE
    nt = rows // tt
    row = pl.BlockSpec((tt, d), lambda i: (i, 0))
    vec = pl.BlockSpec((1, d), lambda i: (0, 0))
    return pl.pallas_call(
        functools.partial(_combine_ln_body, alpha=alpha), grid=(nt,),
        in_specs=[pl.BlockSpec((1, 1, tt * TOP_K), lambda i: (i, 0, 0), memory_space=pltpu.SMEM),
                  pl.BlockSpec(memory_space=pl.ANY),
                  pl.BlockSpec((tt, LANES), lambda i: (i, 0)), row, vec, vec],
        out_specs=row,
        out_shape=jax.ShapeDtypeStruct((rows, d), F32),
        scratch_shapes=[pltpu.VMEM((TOP_K, tt, d), F32), pltpu.SemaphoreType.DMA((TOP_K,))],
        compiler_params=_cparams(("arbitrary",)),
        name="moe_combine_ln2",
    )(dest.reshape(nt, 1, tt * TOP_K), y, gate_pad, h32, g.reshape(1, d), b.reshape(1, d))


def _moe(h32, n_tok, router_w_l, router_b_l, w_gu_l, b_gu_l, w_dn_l, b_dn_l):
    rows = h32.shape[0]
    idx_pad, gate_pad = _router(h32, router_w_l, router_b_l)
    top_idx = idx_pad[:n_tok, :TOP_K]
    gate = gate_pad[:n_tok, :TOP_K]
    n_assign = n_tok * TOP_K
    n_blocks = -(-n_assign // MOE_BLK) + N_EXPERTS
    n_slots = n_blocks * MOE_BLK
    flat_e = top_idx.reshape(-1)
    onehot = (flat_e[:, None] == jnp.arange(N_EXPERTS, dtype=jnp.int32)[None, :]).astype(jnp.int32)
    counts = jnp.sum(onehot, axis=0)
    padded = (counts + MOE_BLK - 1) // MOE_BLK * MOE_BLK
    pad_end = jnp.cumsum(padded)
    pad_start = pad_end - padded
    dest = jnp.sum((jnp.cumsum(onehot, axis=0) - onehot + pad_start[None, :]) * onehot, axis=1)
    slot_tok = jnp.zeros((n_slots,), jnp.int32).at[dest].set(jnp.arange(n_assign, dtype=jnp.int32) // TOP_K)
    n_used = pad_end[-1] // MOE_BLK
    blk = jnp.minimum(jnp.arange(n_blocks, dtype=jnp.int32), n_used - 1)
    block_e = jnp.minimum(jnp.sum((pad_end[None, :] <= (blk * MOE_BLK)[:, None]).astype(jnp.int32), axis=1),
                          N_EXPERTS - 1)
    first = jnp.concatenate([jnp.ones((1,), jnp.int32), (block_e[1:] != block_e[:-1]).astype(jnp.int32)])
    seg_rows = jnp.clip(counts[block_e] - (blk * MOE_BLK - pad_start[block_e]), 0, MOE_BLK)
    nsub = jnp.where(jnp.arange(n_blocks) < n_used, (seg_rows + MOE_SUB - 1) // MOE_SUB, 0).astype(jnp.int32)
    seg_idx = jnp.cumsum(first) - 1
    nseg = seg_idx[-1] + 1
    seg_expert = jnp.zeros((n_blocks + 1,), jnp.int32).at[jnp.where(first == 1, seg_idx, n_blocks)].set(block_e)
    next_expert = jnp.where(seg_idx + 1 < nseg, seg_expert[jnp.minimum(seg_idx + 1, n_blocks - 1)], block_e[0])
    sched = jnp.stack([block_e, first, nsub, seg_idx, next_expert]).astype(jnp.int32)
    xs = _moe_gather(h32, slot_tok, nsub)
    a = _gmm1(xs, w_gu_l, b_gu_l, sched, nseg.reshape(1).astype(jnp.int32))
    y = _gmm2(a, w_dn_l, b_dn_l, sched, nseg.reshape(1).astype(jnp.int32))
    dest_pad = jnp.pad(dest.reshape(n_tok, TOP_K), ((0, rows - n_tok), (0, 0)))
    row_is_token = (jnp.arange(rows) < n_tok)[:, None]
    return y, dest_pad, jnp.where(row_is_token, gate_pad, 0.0)


def _attention_bias_tables(rel_bias):
    qi = jnp.arange(Q_BLOCK)[:, None]
    kj = jnp.arange(2 * Q_BLOCK)[None, :]
    rel = qi + Q_BLOCK - kj
    prompt_tabs, cache_tabs, new_tabs = [], [], []
    buckets = jnp.arange(NUM_BUCKETS, dtype=jnp.int32)
    for gi, (win, dil) in enumerate(GROUPS):
        steps = win // dil
        bias_g = rel_bias[:, gi * H_B:(gi + 1) * H_B].astype(F32)
        hit = _t5_bucket(jnp.maximum(rel, 0) * dil)[None, :, :, None] == buckets
        tab = jnp.sum(jnp.where(hit, bias_g.T[:, None, None, :], 0.0), axis=-1)
        prompt_tabs.append(jnp.where(((rel >= 0) & (rel <= steps))[None], tab, -jnp.inf))
        bj = bias_g[_t5_bucket(jnp.arange(steps + 1) * dil)]
        cache_tabs.append(bj[:0:-1, :, None])
        new_tabs.append(bj[0, :, None])
    return prompt_tabs, jnp.stack(cache_tabs), jnp.stack(new_tabs)


def kernel(x_prompt, x_sample, cache_kv_w128, cache_kv_w512, cache_kv_w2048, state_conv, state_delta, rel_bias,
           w_in, b_gate, conv_w, a_log, dt_bias, o_norm_w, w_branch_a, w_branch_b, w_out,
           ln1_g, ln1_b, router_w, router_b, w_gu, b_gu, w_dn, b_dn, ln2_g, ln2_b):
    batch, seq, d = x_prompt.shape
    n_s = x_sample.shape[0]
    n_p = batch * seq
    n_tok = n_p + n_s
    rows = 8 * ROW_TILE
    depth = w_in.shape[0]
    alpha = (2.0 * depth) ** 0.25
    kv_caches = (cache_kv_w128, cache_kv_w512, cache_kv_w2048)
    hw = H_B * DH_B

    x = jnp.concatenate([x_prompt.reshape(n_p, d), x_sample.reshape(n_s, d),
                         jnp.zeros((rows - n_tok, d), F32)], axis=0)
    prompt_tabs, bias_cache, bias_new = _attention_bias_tables(rel_bias)

    kv_p = [[] for _ in GROUPS]
    kv_s = [[] for _ in GROUPS]
    conv_p, conv_s, delta_p, delta_s = [], [], [], []
    for l in range(depth):
        xb = x.astype(BF16)
        w_in_t = jnp.swapaxes(w_in, 1, 2)
        in_proj = functools.partial(_mm, xb, w_in_t, layer=l, tm=ROW_TILE, w_is_nk=True)
        p_a = in_proj(n_cols=OFF_BETA, tn=512, name="in_proj_deltanet")
        p_ba = in_proj(n_cols=LANES, col0=OFF_BETA, tn=LANES, name="in_proj_beta_alpha")
        p_b = in_proj(n_cols=N_GROUPS * QKV_B, col0=OFF_B, tn=512, name="in_proj_attention")
        gates = in_proj(n_cols=2 * D_MODEL, col0=OFF_GATE, tn=512,
                        bias=b_gate[l].reshape(1, 2 * D_MODEL), name="in_proj_gates")

        beta, g, gc = _beta_g(p_ba, a_log[l], dt_bias[l])
        qkv = _conv_qkv_prompt(p_a, conv_w[l], n_p, seq)
        ya_p, s_p = _gdn_prompt(qkv, p_a, beta[:n_p], gc[:n_p], o_norm_w[l], batch, seq)
        ya_s, cs, s_s = _gdn_sample(p_a[n_p:n_tok].reshape(n_s, 1, OFF_BETA), state_conv[l], conv_w[l],
                                    beta[n_p:n_tok], g[n_p:n_tok], state_delta[l], o_norm_w[l])
        ya = jnp.concatenate([ya_p, ya_s.reshape(n_s, H_A * DV_A),
                              jnp.zeros((rows - n_tok, H_A * DV_A), BF16)], axis=0)

        outs, lses = [], []
        for gi in range(N_GROUPS):
            o_g, lse_g = _attn_prompt(p_b, gi, prompt_tabs[gi], batch, seq)
            outs.append(o_g)
            lses.append(lse_g)
        yb_p = _merge_groups(outs, lses)
        p_b_s = p_b[n_p:n_tok]
        yb_s = _attn_sample(p_b_s.reshape(n_s, 3 * N_GROUPS, H_B, DH_B), [c[l] for c in kv_caches],
                            bias_cache, bias_new)
        yb = jnp.concatenate([yb_p, yb_s.reshape(n_s, hw), jnp.zeros((rows - n_tok, hw), BF16)], axis=0)

        merged = _branch_merge(ya, yb, w_branch_a[l], w_branch_b[l], gates)
        mix = _mm(merged, w_out, layer=l, n_cols=D_MODEL, tm=ROW_TILE, tn=512, name="out_proj")
        (h32,) = _residual_ln(x, mix, ln1_g[l], ln1_b[l], alpha, (F32,), "deepnorm_ln1")
        y_slots, dest, gate = _moe(h32, n_tok, router_w[l], router_b[l], w_gu[l], b_gu[l], w_dn[l], b_dn[l])
        x = _combine_ln(h32, y_slots, dest, gate, ln2_g[l], ln2_b[l], alpha)

        new_rows = []
        for gi, (win, dil) in enumerate(GROUPS):
            keep = min(win, seq)
            c0, c1 = gi * QKV_B + hw, (gi + 1) * QKV_B
            last = [p_b[(b + 1) * seq - keep:(b + 1) * seq, c0:c1] for b in range(batch)]
            kv_p[gi].append(jnp.stack(last).reshape(batch, keep, 2, H_B, DH_B))
            new_rows.append(p_b_s[:, c0:c1].reshape(n_s, 1, 2, H_B, DH_B))
        for gi, shifted in enumerate(_cache_shift([c[l] for c in kv_caches], new_rows)):
            kv_s[gi].append(shifted)
        conv_p.append(jnp.stack([p_a[(b + 1) * seq - (CONV_W - 1):(b + 1) * seq, :CONV_DIM] for b in range(batch)]))
        conv_s.append(cs)
        delta_p.append(s_p)
        delta_s.append(s_s)

    return (x[:n_p].reshape(batch, seq, d), x[n_p:n_tok].reshape(n_s, 1, d),
            jnp.stack(kv_p[0]), jnp.stack(kv_s[0]),
            jnp.stack(kv_p[1]), jnp.stack(kv_s[1]),
            jnp.stack(kv_p[2]), jnp.stack(kv_s[2]),
            jnp.stack(conv_p), jnp.stack(conv_s),
            jnp.stack(delta_p), jnp.stack(delta_s))
```

```python
import functools
import math

import jax
import jax.numpy as jnp
from jax import lax
from jax.experimental import pallas as pl
from jax.experimental.pallas import tpu as pltpu

F32 = jnp.float32
BF16 = jnp.bfloat16

D_MODEL = 4096
H_A, DK_A, DV_A = 16, 128, 128
CONV_W = 4
CONV_DIM = H_A * (2 * DK_A + DV_A)
CHUNK = 64
H_B, DH_B = 8, 128
GROUPS = ((128, 1), (512, 4), (2048, 16))
N_GROUPS = len(GROUPS)
Q_BLOCK = 128
NUM_BUCKETS, MAX_DISTANCE = 32, 2048
N_EXPERTS, TOP_K = 32, 4
D_FF = D_MODEL
SWIGLU_LIMIT, SWIGLU_ALPHA = 7.0, 1.702
LN_EPS, RMS_EPS = 1e-5, 1e-6

OFF_Z = CONV_DIM
OFF_BETA = OFF_Z + H_A * DV_A
OFF_ALPHA = OFF_BETA + H_A
OFF_B = OFF_ALPHA + H_A
QKV_B = 3 * H_B * DH_B
OFF_GATE = OFF_B + N_GROUPS * QKV_B

LANES = 128
VMEM_LIMIT = 56 * 1024 * 1024
ROW_TILE = 1032
MOE_BLK = 512
MOE_SUB = 128


def _cparams(sem):
    return pltpu.CompilerParams(dimension_semantics=sem, vmem_limit_bytes=VMEM_LIMIT)


def _dot(a, b):
    return jnp.dot(a.astype(BF16), b.astype(BF16), preferred_element_type=F32)


def _dot_nt(a, b):
    return lax.dot_general(a.astype(BF16), b.astype(BF16), (((1,), (1,)), ((), ())),
                           preferred_element_type=F32)


def _dot_tn(a, b):
    return lax.dot_general(a.astype(BF16), b.astype(BF16), (((0,), (0,)), ((), ())),
                           preferred_element_type=F32)


def _dot_hi(a, b):
    return jnp.dot(a, b, precision=lax.Precision.HIGHEST, preferred_element_type=F32)


def _silu(x):
    return x * jax.nn.sigmoid(x)


def _rb(x):
    return x.astype(BF16).astype(F32)


def _mm_body(*refs, sigmoid_bias, w_is_nk):
    if sigmoid_bias:
        x_ref, w_ref, b_ref, o_ref = refs
    else:
        x_ref, w_ref, o_ref = refs
    w = w_ref[0].astype(BF16)
    contract = (((1,), (1,)), ((), ())) if w_is_nk else (((1,), (0,)), ((), ()))
    acc = lax.dot_general(x_ref[...], w, contract, preferred_element_type=F32)
    if sigmoid_bias:
        acc = jax.nn.sigmoid(acc + b_ref[...])
    o_ref[...] = acc.astype(o_ref.dtype)


def _mm(x, w, *, layer=0, n_cols, col0=0, tm, tn, w_is_nk=False, bias=None, out_dtype=F32, name):
    m, k = x.shape
    if w_is_nk:
        w_spec = pl.BlockSpec((pl.Element(1), pl.Element(tn), pl.Element(k)),
                              lambda i, j: (layer, pl.multiple_of(col0 + j * tn, 8), 0))
    else:
        w_spec = pl.BlockSpec((1, k, tn), lambda i, j: (layer, 0, j + col0 // tn))
    in_specs = [pl.BlockSpec((tm, k), lambda i, j: (i, 0)), w_spec]
    args = [x, w]
    if bias is not None:
        in_specs.append(pl.BlockSpec((1, tn), lambda i, j: (0, j)))
        args.append(bias)
    return pl.pallas_call(
        functools.partial(_mm_body, sigmoid_bias=bias is not None, w_is_nk=w_is_nk),
        grid=(m // tm, n_cols // tn),
        in_specs=in_specs,
        out_specs=pl.BlockSpec((tm, tn), lambda i, j: (i, j)),
        out_shape=jax.ShapeDtypeStruct((m, n_cols), out_dtype),
        compiler_params=_cparams(("parallel", "parallel")),
        name=name,
    )(*args)


def _bg_body(p_ref, a_ref, dt_ref, beta_ref, g_ref, gc_ref):
    p = p_ref[...]
    tt = p.shape[0]
    beta_ref[...] = jax.nn.sigmoid(p)
    g = -jnp.exp(a_ref[...]) * jax.nn.softplus(p + dt_ref[...])
    g_ref[...] = g
    ri = lax.broadcasted_iota(jnp.int32, (tt, tt), 0)
    ci = lax.broadcasted_iota(jnp.int32, (tt, tt), 1)
    same_chunk_before = ((ri // CHUNK) == (ci // CHUNK)) & (ci <= ri)
    gc_ref[...] = _dot_hi(same_chunk_before.astype(F32), g)


def _beta_g(p_ba, a_log_l, dt_bias_l):
    rows = p_ba.shape[0]
    tt = 3 * CHUNK
    zeros = jnp.zeros((LANES - 2 * H_A,), F32)
    a_vec = jnp.concatenate([jnp.zeros((H_A,), F32), a_log_l, zeros]).reshape(1, LANES)
    dt_vec = jnp.concatenate([jnp.zeros((H_A,), F32), dt_bias_l, zeros]).reshape(1, LANES)
    row_spec = pl.BlockSpec((tt, LANES), lambda i: (i, 0))
    vec_spec = pl.BlockSpec((1, LANES), lambda i: (0, 0))
    shp = jax.ShapeDtypeStruct((rows, LANES), F32)
    beta, g, gc = pl.pallas_call(
        _bg_body, grid=(rows // tt,),
        in_specs=[row_spec, vec_spec, vec_spec],
        out_specs=[row_spec, row_spec, row_spec],
        out_shape=[shp, shp, shp],
        compiler_params=_cparams(("parallel",)),
        name="deltanet_gates",
    )(p_ba, a_vec, dt_vec)
    return beta[:, :H_A], g[:, H_A:2 * H_A], gc[:, H_A:2 * H_A]


def _conv_body(cur_ref, halo_ref, w_ref, o_ref, *, seq, tt):
    i = pl.program_id(0)
    j = pl.program_id(1)
    cur = cur_ref[...]
    halo = jnp.where((i * tt) % seq == 0, 0.0, halo_ref[...])
    xh = jnp.concatenate([halo, cur], axis=0)
    w = w_ref[...]
    acc = xh[5:5 + tt] * w[0:1]
    for t in range(1, CONV_W):
        acc = acc + xh[5 + t:5 + t + tt] * w[t:t + 1]
    y = _silu(acc)
    ct = y.shape[1]
    n_qk_blocks = 2 * H_A * DK_A // ct
    is_q = j < H_A * DK_A // ct
    is_qk = j < n_qk_blocks
    scale = jnp.where(is_q, DK_A ** -0.5, 1.0)
    for h in range(ct // DK_A):
        sl = slice(h * DK_A, (h + 1) * DK_A)
        yh = y[:, sl]
        nrm = yh * lax.rsqrt(jnp.sum(yh * yh, axis=-1, keepdims=True) + RMS_EPS) * scale
        o_ref[:, sl] = jnp.where(is_qk, nrm, yh)


def _conv_qkv_prompt(p_a, conv_w_l, n_rows, seq):
    tt, ct = 256, 1024
    return pl.pallas_call(
        functools.partial(_conv_body, seq=seq, tt=tt),
        grid=(n_rows // tt, CONV_DIM // ct),
        in_specs=[pl.BlockSpec((tt, ct), lambda i, j: (i, j)),
                  pl.BlockSpec((8, ct), lambda i, j: (jnp.maximum(i * (tt // 8) - 1, 0), j)),
                  pl.BlockSpec((CONV_W, ct), lambda i, j: (0, j))],
        out_specs=pl.BlockSpec((tt, ct), lambda i, j: (i, j)),
        out_shape=jax.ShapeDtypeStruct((n_rows, CONV_DIM), F32),
        compiler_params=_cparams(("parallel", "parallel")),
        name="conv_silu_l2norm",
    )(p_a, p_a, conv_w_l)


def _split_bf16(a):
    hi = a.astype(BF16)
    return hi, (a - hi.astype(F32)).astype(BF16)


def _dot_3pass(a, b):
    a_hi, a_lo = _split_bf16(a)
    b_hi, b_lo = _split_bf16(b)
    return (jnp.dot(a_hi, b_hi, preferred_element_type=F32) + jnp.dot(a_hi, b_lo, preferred_element_type=F32)
            + jnp.dot(a_lo, b_hi, preferred_element_type=F32))


def _gdn_wy_body(q_ref, k_ref, v_ref, b_ref, gc_ref, gr_ref, u_ref, w_ref, qg_ref, kd_ref, attn_ref, *, hb):
    C = CHUNK
    ri = lax.broadcasted_iota(jnp.int32, (C, C), 0)
    ci = lax.broadcasted_iota(jnp.int32, (C, C), 1)
    tril = ci <= ri
    strict = ci < ri
    ys, zs = [], []
    for hh in range(hb):
        sl = slice(hh * DK_A, (hh + 1) * DK_A)
        q = q_ref[:, sl]
        k = k_ref[:, sl]
        v = v_ref[:, sl]
        b = b_ref[0, 0, 0][:, hh:hh + 1]
        gc = gc_ref[0, 0, 0][:, hh:hh + 1]
        gr = gr_ref[0, 0, 0][hh:hh + 1, :]
        decay = jnp.where(tril, jnp.exp(jnp.where(tril, gc - gr, 0.0)), 0.0)
        kb = k * b
        eg = jnp.exp(gc)
        ys.append(-jnp.where(strict, _dot_nt(kb, k) * decay, 0.0))
        zs.append(jnp.concatenate([v * b, kb * eg], axis=-1))
        qg_ref[:, sl] = (q * eg).astype(qg_ref.dtype)
        kd_ref[:, sl] = (k * jnp.exp(gc[C - 1:C, :] - gc)).astype(kd_ref.dtype)
        attn_ref[0, hh] = jnp.where(tril, _dot_nt(q, k) * decay, 0.0).astype(attn_ref.dtype)
    for step in range(6):
        for hh in range(hb):
            y, z = ys[hh], zs[hh]
            if step < 5:
                prod = _dot_3pass(y, jnp.concatenate([z, y], axis=-1))
                zs[hh] = z + prod[:, :DV_A + DK_A]
                ys[hh] = prod[:, DV_A + DK_A:]
            else:
                zs[hh] = z + _dot_3pass(y, z)
    for hh in range(hb):
        sl = slice(hh * DK_A, (hh + 1) * DK_A)
        u_ref[:, sl] = zs[hh][:, :DV_A]
        w_ref[:, sl] = zs[hh][:, DV_A:].astype(w_ref.dtype)


def _gdn_scan_body(u_ref, w_ref, qg_ref, kd_ref, attn_ref, gl_ref, z_ref, nw_ref, ya_ref, s_ref):
    c = pl.program_id(1)

    @pl.when(c == 0)
    def _():
        s_ref[...] = jnp.zeros_like(s_ref)

    nw = nw_ref[...]
    v_news, o_states = [], []
    for hh in range(H_A):
        sl = slice(hh * DK_A, (hh + 1) * DK_A)
        s_bf = s_ref[0, hh].astype(BF16)
        v_news.append((u_ref[:, sl] - jnp.dot(w_ref[:, sl], s_bf, preferred_element_type=F32)).astype(BF16))
        o_states.append(jnp.dot(qg_ref[:, sl], s_bf, preferred_element_type=F32))
    for hh in range(H_A):
        sl = slice(hh * DK_A, (hh + 1) * DK_A)
        s = s_ref[0, hh]
        v_bf = v_news[hh]
        o = o_states[hh] + jnp.dot(attn_ref[0, hh], v_bf, preferred_element_type=F32)
        decay_last = jnp.exp(gl_ref[0, 0, hh:hh + 1, :])
        s_ref[0, hh] = s * decay_last + lax.dot_general(kd_ref[:, sl], v_bf, (((0,), (0,)), ((), ())),
                                                         preferred_element_type=F32)
        o = o * lax.rsqrt(jnp.mean(o * o, axis=-1, keepdims=True) + RMS_EPS) * nw
        ya_ref[:, sl] = (o * _silu(z_ref[:, sl])).astype(ya_ref.dtype)


def _gdn_prompt(qkv, p_a, beta, gc, o_norm_w_l, batch, seq):
    hb = 8
    nhb = H_A // hb
    nc = seq // CHUNK
    w = hb * DK_A
    hw = H_A * DK_A
    n = batch * seq

    def per_head_cols(a):
        return a.reshape(batch, nc, CHUNK, nhb, hb).transpose(0, 3, 1, 2, 4)

    beta_c = per_head_cols(beta)
    gc_c = per_head_cols(gc)
    gc_r = gc_c.transpose(0, 1, 2, 4, 3)
    col_spec = pl.BlockSpec((1, 1, 1, CHUNK, hb), lambda b, h, c: (b, h, c, 0, 0))
    row_spec = pl.BlockSpec((1, 1, 1, hb, CHUNK), lambda b, h, c: (b, h, c, 0, 0))

    def tok_spec(col_block0):
        return pl.BlockSpec((CHUNK, w), lambda b, h, c: (b * nc + c, col_block0 + h))

    attn_spec = pl.BlockSpec((1, hb, CHUNK, CHUNK), lambda b, h, c: (b * nc + c, h, 0, 0))
    u, wf, qg, kd, attn = pl.pallas_call(
        functools.partial(_gdn_wy_body, hb=hb),
        grid=(batch, nhb, nc),
        in_specs=[tok_spec(0), tok_spec(hw // w), tok_spec(2 * hw // w), col_spec, col_spec, row_spec],
        out_specs=[tok_spec(0), tok_spec(0), tok_spec(0), tok_spec(0), attn_spec],
        out_shape=[jax.ShapeDtypeStruct((n, hw), F32), jax.ShapeDtypeStruct((n, hw), BF16),
                   jax.ShapeDtypeStruct((n, hw), BF16), jax.ShapeDtypeStruct((n, hw), BF16),
                   jax.ShapeDtypeStruct((batch * nc, H_A, CHUNK, CHUNK), BF16)],
        compiler_params=_cparams(("parallel", "parallel", "parallel")),
        name="gated_delta_wy",
    )(qkv, qkv, qkv, beta_c, gc_c, gc_r)

    g_last = jnp.broadcast_to(gc.reshape(batch, nc, CHUNK, H_A)[:, :, CHUNK - 1, :, None], (batch, nc, H_A, LANES))
    row = pl.BlockSpec((CHUNK, hw), lambda b, c: (b * nc + c, 0))
    ya, s_fin = pl.pallas_call(
        _gdn_scan_body,
        grid=(batch, nc),
        in_specs=[row, row, row, row,
                  pl.BlockSpec((1, H_A, CHUNK, CHUNK), lambda b, c: (b * nc + c, 0, 0, 0)),
                  pl.BlockSpec((1, 1, H_A, LANES), lambda b, c: (b, c, 0, 0)),
                  pl.BlockSpec((CHUNK, hw), lambda b, c: (b * nc + c, OFF_Z // hw)),
                  pl.BlockSpec((1, DV_A), lambda b, c: (0, 0))],
        out_specs=[row, pl.BlockSpec((1, H_A, DK_A, DV_A), lambda b, c: (b, 0, 0, 0))],
        out_shape=[jax.ShapeDtypeStruct((n, hw), BF16),
                   jax.ShapeDtypeStruct((batch, H_A, DK_A, DV_A), F32)],
        compiler_params=_cparams(("parallel", "arbitrary")),
        name="gated_delta_scan",
    )(u, wf, qg, kd, attn, g_last, p_a, o_norm_w_l.reshape(1, DV_A))
    return ya, s_fin


def _gdn_sample_body(pa_ref, cbuf_ref, cw_ref, bg_ref, s_ref, nw_ref, ya_ref, conv_ref, so_ref):
    new = pa_ref[0, :, :CONV_DIM]
    buf = cbuf_ref[0]
    w = cw_ref[...]
    acc = buf[0:1] * w[0:1]
    for t in range(1, CONV_W - 1):
        acc = acc + buf[t:t + 1] * w[t:t + 1]
    y = _silu(acc + new * w[CONV_W - 1:CONV_W])
    conv_ref[0, 0:2, :] = buf[1:3]
    conv_ref[0, 2:3, :] = new
    z = pa_ref[0, :, OFF_Z:OFF_BETA]
    bg = bg_ref[0]
    nw = nw_ref[...]
    ri = lax.broadcasted_iota(jnp.int32, (DK_A, DK_A), 0)
    ci = lax.broadcasted_iota(jnp.int32, (DK_A, DK_A), 1)
    eye = ri == ci

    def as_col(row):
        return jnp.sum(jnp.where(eye, row, 0.0), axis=1, keepdims=True)

    for h in range(H_A):
        qh = y[:, h * DK_A:(h + 1) * DK_A]
        kh = y[:, (H_A + h) * DK_A:(H_A + h + 1) * DK_A]
        vh = y[:, (2 * H_A + h) * DK_A:(2 * H_A + h + 1) * DK_A]
        qh = qh * lax.rsqrt(jnp.sum(qh * qh, axis=-1, keepdims=True) + RMS_EPS) * (DK_A ** -0.5)
        kh = kh * lax.rsqrt(jnp.sum(kh * kh, axis=-1, keepdims=True) + RMS_EPS)
        beta = bg[0:1, h:h + 1]
        g = bg[1:2, h:h + 1]
        k_col = as_col(kh)
        s = s_ref[0, h] * jnp.exp(g)
        kv = jnp.sum(_rb(s) * _rb(k_col), axis=0, keepdims=True)
        s = s + k_col * ((vh - kv) * beta)
        so_ref[0, h] = s
        o = jnp.sum(_rb(s) * _rb(as_col(qh)), axis=0, keepdims=True)
        o = o * lax.rsqrt(jnp.mean(o * o, axis=-1, keepdims=True) + RMS_EPS) * nw
        ya_ref[0, :, h * DV_A:(h + 1) * DV_A] = (o * _silu(z[:, h * DV_A:(h + 1) * DV_A])).astype(ya_ref.dtype)


def _gdn_sample(p_a_s, conv_buf, conv_w_l, beta_s, g_s, s_delta, o_norm_w_l):
    n = p_a_s.shape[0]
    bg = jnp.stack([beta_s, g_s], axis=1)
    return pl.pallas_call(
        _gdn_sample_body, grid=(n,),
        in_specs=[pl.BlockSpec((1, 1, OFF_BETA), lambda i: (i, 0, 0)),
                  pl.BlockSpec((1, CONV_W - 1, CONV_DIM), lambda i: (i, 0, 0)),
                  pl.BlockSpec((CONV_W, CONV_DIM), lambda i: (0, 0)),
                  pl.BlockSpec((1, 2, H_A), lambda i: (i, 0, 0)),
                  pl.BlockSpec((1, H_A, DK_A, DV_A), lambda i: (i, 0, 0, 0)),
                  pl.BlockSpec((1, DV_A), lambda i: (0, 0))],
        out_specs=[pl.BlockSpec((1, 1, H_A * DV_A), lambda i: (i, 0, 0)),
                   pl.BlockSpec((1, CONV_W - 1, CONV_DIM), lambda i: (i, 0, 0)),
                   pl.BlockSpec((1, H_A, DK_A, DV_A), lambda i: (i, 0, 0, 0))],
        out_shape=[jax.ShapeDtypeStruct((n, 1, H_A * DV_A), BF16),
                   jax.ShapeDtypeStruct((n, CONV_W - 1, CONV_DIM), F32),
                   jax.ShapeDtypeStruct((n, H_A, DK_A, DV_A), F32)],
        compiler_params=_cparams(("parallel",)),
        name="gated_delta_step",
    )(p_a_s, conv_buf, conv_w_l, bg, s_delta, o_norm_w_l.reshape(1, DV_A))


def _t5_bucket(dist):
    max_exact = NUM_BUCKETS // 2
    d = jnp.maximum(dist, 1).astype(F32)
    large = max_exact + (jnp.log(d / max_exact) / math.log(MAX_DISTANCE / max_exact)
                         * (NUM_BUCKETS - max_exact)).astype(jnp.int32)
    large = jnp.minimum(large, NUM_BUCKETS - 1)
    return jnp.where(dist < max_exact, dist, large)


def _attn_prompt_body(q_ref, kp_ref, kc_ref, vp_ref, vc_ref, bias_ref, o_ref, lse_ref, *, dil, hc):
    n = pl.program_id(1)
    hg = pl.program_id(2)
    kj = lax.broadcasted_iota(jnp.int32, (Q_BLOCK, 2 * Q_BLOCK), 1)
    in_seq = (n > 0) | (kj >= Q_BLOCK)
    units = [(pl.ds(r, Q_BLOCK, stride=dil) if dil > 1 else slice(None), h)
             for r in range(dil) for h in range(hc)]
    scores = []
    for rows, h in units:
        sl = slice(h * DH_B, (h + 1) * DH_B)
        kk = jnp.concatenate([kp_ref[rows, sl], kc_ref[rows, sl]], axis=0)
        scores.append(_dot_nt(q_ref[rows, sl], kk))
    for (rows, h), sc in zip(units, scores):
        sl = slice(h * DH_B, (h + 1) * DH_B)
        vv = jnp.concatenate([vp_ref[rows, sl], vc_ref[rows, sl]], axis=0)
        logits = jnp.where(in_seq, sc * (DH_B ** -0.5) + bias_ref[hg * hc + h], -jnp.inf)
        mx = jnp.max(logits, axis=-1, keepdims=True)
        e = jnp.exp(logits - mx)
        den = jnp.sum(e, axis=-1, keepdims=True)
        o_ref[rows, sl] = _dot(e / den, vv)
        lse_ref[rows, sl] = jnp.broadcast_to(mx + jnp.log(den), (Q_BLOCK, DH_B))


def _attn_prompt(p_b, gi, bias_tab, batch, seq):
    _, dil = GROUPS[gi]
    rb = Q_BLOCK * dil
    nb = seq // rb
    hc = H_B if dil == 1 else 1
    cw = hc * DH_B
    hw = H_B * DH_B

    def spec(which, prev):
        def imap(b, n, h):
            nn = jnp.maximum(n - 1, 0) if prev else n
            return (b * nb + nn, (gi * QKV_B + which * hw) // cw + h)
        return pl.BlockSpec((rb, cw), imap)

    out_spec = pl.BlockSpec((rb, cw), lambda b, n, h: (b * nb + n, h))
    shp = jax.ShapeDtypeStruct((batch * seq, hw), F32)
    return pl.pallas_call(
        functools.partial(_attn_prompt_body, dil=dil, hc=hc), grid=(batch, nb, H_B // hc),
        in_specs=[spec(0, False), spec(1, True), spec(1, False), spec(2, True), spec(2, False),
                  pl.BlockSpec((H_B, Q_BLOCK, 2 * Q_BLOCK), lambda b, n, h: (0, 0, 0))],
        out_specs=[out_spec, out_spec],
        out_shape=[shp, shp],
        compiler_params=_cparams(("parallel", "parallel", "parallel")),
        name=f"dilated_attn_prompt_g{gi}",
    )(p_b, p_b, p_b, p_b, p_b, bias_tab)


def _merge_body(o0, o1, o2, l0, l1, l2, yb_ref):
    ls = [l0[...], l1[...], l2[...]]
    mx = jnp.maximum(jnp.maximum(ls[0], ls[1]), ls[2])
    es = [jnp.exp(l - mx) for l in ls]
    den = es[0] + es[1] + es[2]
    acc = (_rb(es[0] / den) * _rb(o0[...]) + _rb(es[1] / den) * _rb(o1[...])
           + _rb(es[2] / den) * _rb(o2[...]))
    yb_ref[...] = acc.astype(yb_ref.dtype)


def _merge_groups(outs, lses):
    rows, hw = outs[0].shape
    tt = 512
    spec = pl.BlockSpec((tt, hw), lambda i: (i, 0))
    return pl.pallas_call(
        _merge_body, grid=(rows // tt,),
        in_specs=[spec] * 6, out_specs=spec,
        out_shape=jax.ShapeDtypeStruct((rows, hw), BF16),
        compiler_params=_cparams(("parallel",)),
        name="merge_attention_groups",
    )(*outs, *lses)


def _attn_sample_body(pb_ref, c0_ref, c1_ref, c2_ref, bc_ref, bn_ref, yb_ref):
    caches = (c0_ref, c1_ref, c2_ref)
    scale = DH_B ** -0.5
    outs, lses = [], []
    for gi in range(N_GROUPS):
        q = pb_ref[0, 3 * gi]
        k_new = pb_ref[0, 3 * gi + 1]
        v_new = pb_ref[0, 3 * gi + 2]
        kc = caches[gi][0, :, 0, 0]
        vc = caches[gi][0, :, 0, 1]
        qr = _rb(q)
        lc = jnp.sum(_rb(kc) * qr[None], axis=-1, keepdims=True) * scale + bc_ref[gi]
        ln = jnp.sum(_rb(k_new) * qr, axis=-1, keepdims=True) * scale + bn_ref[gi]
        mx = jnp.maximum(jnp.max(lc, axis=0), ln)
        ec = jnp.exp(lc - mx[None])
        en = jnp.exp(ln - mx)
        den = jnp.sum(ec, axis=0) + en
        outs.append(jnp.sum(_rb(ec / den[None]) * _rb(vc), axis=0) + _rb(en / den) * _rb(v_new))
        lses.append(mx + jnp.log(den))
    mx = jnp.maximum(jnp.maximum(lses[0], lses[1]), lses[2])
    es = [jnp.exp(l - mx) for l in lses]
    den = es[0] + es[1] + es[2]
    acc = _rb(es[0] / den) * _rb(outs[0]) + _rb(es[1] / den) * _rb(outs[1]) + _rb(es[2] / den) * _rb(outs[2])
    yb_ref[0] = acc.astype(yb_ref.dtype)


def _attn_sample(p_b_s, caches, bias_cache, bias_new):
    n = p_b_s.shape[0]
    views, specs = [], []
    for (win, dil), c in zip(GROUPS, caches):
        wb = c.shape[1]
        views.append(c.reshape(n, wb // dil, dil, 2, H_B, DH_B))
        specs.append(pl.BlockSpec((1, wb // dil, 1, 2, H_B, DH_B), lambda i: (i, 0, 0, 0, 0, 0)))
    return pl.pallas_call(
        _attn_sample_body, grid=(n,),
        in_specs=[pl.BlockSpec((1, 3 * N_GROUPS, H_B, DH_B), lambda i: (i, 0, 0, 0))] + specs
                 + [pl.BlockSpec((N_GROUPS, Q_BLOCK, H_B, 1), lambda i: (0, 0, 0, 0)),
                    pl.BlockSpec((N_GROUPS, H_B, 1), lambda i: (0, 0, 0))],
        out_specs=pl.BlockSpec((1, H_B, DH_B), lambda i: (i, 0, 0)),
        out_shape=jax.ShapeDtypeStruct((n, H_B, DH_B), BF16),
        compiler_params=_cparams(("parallel",)),
        name="dilated_attn_sample",
    )(p_b_s, *views, bias_cache, bias_new)


CACHE_SHIFT_ROWS = 512


def _cache_shift_body(cur_ref, nxt_ref, new_ref, o_ref):
    i = pl.program_id(1)
    t = cur_ref.shape[1]
    o_ref[0, 0:t - 1] = cur_ref[0, 1:t]

    @pl.when(i < pl.num_programs(1) - 1)
    def _():
        o_ref[0, t - 1:t] = nxt_ref[0]

    @pl.when(i == pl.num_programs(1) - 1)
    def _():
        o_ref[0, t - 1:t] = new_ref[0]


def _cache_shift(caches, new_rows):
    outs = []
    for g, (c, new) in enumerate(zip(caches, new_rows)):
        n, wb = c.shape[0], c.shape[1]
        t = min(wb, CACHE_SHIFT_ROWS)
        tail = c.shape[2:]
        zeros = (0,) * len(tail)
        outs.append(pl.pallas_call(
            _cache_shift_body, grid=(n, wb // t),
            in_specs=[pl.BlockSpec((1, t) + tail, lambda b, i: (b, i) + zeros),
                      pl.BlockSpec((1, 1) + tail, lambda b, i: (b, jnp.minimum((i + 1) * t, wb - 1)) + zeros),
                      pl.BlockSpec((1, 1) + tail, lambda b, i: (b, 0) + zeros)],
            out_specs=pl.BlockSpec((1, t) + tail, lambda b, i: (b, i) + zeros),
            out_shape=jax.ShapeDtypeStruct(c.shape, c.dtype),
            compiler_params=_cparams(("parallel", "arbitrary")),
            name=f"kv_cache_shift_g{g}",
        )(c, c, new))
    return outs


def _branch_body(ya_ref, yb_ref, wa_ref, wb_ref, ga_ref, gb_ref, o_ref):
    ba = jnp.dot(ya_ref[...], wa_ref[...].astype(BF16), preferred_element_type=F32)
    bb = jnp.dot(yb_ref[...], wb_ref[...].astype(BF16), preferred_element_type=F32)
    o_ref[...] = (ga_ref[...] * ba + gb_ref[...] * bb).astype(o_ref.dtype)


def _branch_merge(ya, yb, w_pa, w_pb, gates):
    rows = ya.shape[0]
    tm, tn = ROW_TILE, 512
    nj = D_MODEL // tn
    return pl.pallas_call(
        _branch_body, grid=(rows // tm, nj),
        in_specs=[pl.BlockSpec((tm, ya.shape[1]), lambda i, j: (i, 0)),
                  pl.BlockSpec((tm, yb.shape[1]), lambda i, j: (i, 0)),
                  pl.BlockSpec((w_pa.shape[0], tn), lambda i, j: (0, j)),
                  pl.BlockSpec((w_pb.shape[0], tn), lambda i, j: (0, j)),
                  pl.BlockSpec((tm, tn), lambda i, j: (i, j)),
                  pl.BlockSpec((tm, tn), lambda i, j: (i, nj + j))],
        out_specs=pl.BlockSpec((tm, tn), lambda i, j: (i, j)),
        out_shape=jax.ShapeDtypeStruct((rows, D_MODEL), BF16),
        compiler_params=_cparams(("parallel", "parallel")),
        name="branch_proj_gate",
    )(ya, yb, w_pa, w_pb, gates, gates)


def _ln_body(x_ref, r_ref, g_ref, b_ref, *out_refs, alpha):
    v = alpha * x_ref[...] + r_ref[...]
    mu = jnp.mean(v, axis=-1, keepdims=True)
    var = jnp.mean(jnp.square(v - mu), axis=-1, keepdims=True)
    y = (v - mu) * lax.rsqrt(var + LN_EPS) * g_ref[...] + b_ref[...]
    for o_ref in out_refs:
        o_ref[...] = y.astype(o_ref.dtype)


def _residual_ln(x, r, g, b, alpha, out_dtypes, name):
    rows, d = x.shape
    tt = 344
    spec = pl.BlockSpec((tt, d), lambda i: (i, 0))
    vec = pl.BlockSpec((1, d), lambda i: (0, 0))
    return pl.pallas_call(
        functools.partial(_ln_body, alpha=alpha), grid=(rows // tt,),
        in_specs=[spec, spec, vec, vec],
        out_specs=[spec] * len(out_dtypes),
        out_shape=[jax.ShapeDtypeStruct((rows, d), dt) for dt in out_dtypes],
        compiler_params=_cparams(("parallel",)),
        name=name,
    )(x, r, g.reshape(1, d), b.reshape(1, d))


def _router_body(h_ref, w_ref, b_ref, idx_ref, gate_ref):
    logits = _dot(h_ref[...], w_ref[...]) + b_ref[...]
    lane = lax.broadcasted_iota(jnp.int32, logits.shape, 1)
    vals, idxs = [], []
    for _ in range(TOP_K):
        mx = jnp.max(logits, axis=-1, keepdims=True)
        idx = jnp.min(jnp.where(logits == mx, lane, LANES), axis=-1, keepdims=True)
        vals.append(mx)
        idxs.append(idx)
        logits = jnp.where(lane == idx, -jnp.inf, logits)
    es = [jnp.exp(v - vals[0]) for v in vals]
    den = es[0] + es[1] + es[2] + es[3]
    idx_out = jnp.zeros(logits.shape, jnp.int32)
    gate_out = jnp.zeros(logits.shape, F32)
    for k in range(TOP_K):
        idx_out = jnp.where(lane == k, idxs[k], idx_out)
        gate_out = jnp.where(lane == k, es[k] / den, gate_out)
    idx_ref[...] = idx_out
    gate_ref[...] = gate_out


def _router(h, router_w_l, router_b_l):
    rows, d = h.shape
    tt = 344
    w = jnp.pad(router_w_l, ((0, 0), (0, LANES - N_EXPERTS)))
    b = jnp.concatenate([router_b_l, jnp.full((LANES - N_EXPERTS,), -jnp.inf, F32)]).reshape(1, LANES)
    spec = pl.BlockSpec((tt, LANES), lambda i: (i, 0))
    return pl.pallas_call(
        _router_body, grid=(rows // tt,),
        in_specs=[pl.BlockSpec((tt, d), lambda i: (i, 0)),
                  pl.BlockSpec((d, LANES), lambda i: (0, 0)),
                  pl.BlockSpec((1, LANES), lambda i: (0, 0))],
        out_specs=[spec, spec],
        out_shape=[jax.ShapeDtypeStruct((rows, LANES), jnp.int32),
                   jax.ShapeDtypeStruct((rows, LANES), F32)],
        compiler_params=_cparams(("parallel",)),
        name="router_top4",
    )(h, w, b)


def _expert_weight_copy(w_hbm, wbuf, sem, expert, col, slot, part):
    tn = wbuf.shape[-1]
    return pltpu.make_async_copy(w_hbm.at[expert, :, pl.ds(col, tn)], wbuf.at[slot, part], sem.at[slot, part])


def _expert_weights_step(sched_ref, nseg_ref, w_hbm, wbuf, w_bf, sem, col_of):
    j, i = pl.program_id(0), pl.program_id(1)
    nj = pl.num_programs(0)
    nseg = nseg_ref[0]
    n_parts = wbuf.shape[1]

    @pl.when(sched_ref[1, i] == 1)
    def _():
        seg = sched_ref[3, i]
        slot = (j * nseg + seg) % 2
        is_last_seg = seg == nseg - 1

        @pl.when((j == 0) & (i == 0))
        def _():
            for part in range(n_parts):
                _expert_weight_copy(w_hbm, wbuf, sem, sched_ref[0, 0], col_of(0, part), 0, part).start()

        for part in range(n_parts):
            _expert_weight_copy(w_hbm, wbuf, sem, 0, 0, slot, part).wait()

        @pl.when(jnp.logical_not(is_last_seg & (j == nj - 1)))
        def _():
            nxt_j = jnp.where(is_last_seg, j + 1, j)
            for part in range(n_parts):
                _expert_weight_copy(w_hbm, wbuf, sem, sched_ref[4, i], col_of(nxt_j, part), 1 - slot, part).start()

        for part in range(n_parts):
            w_bf[part] = wbuf[slot, part].astype(BF16)


def _gmm1_body(sched_ref, nseg_ref, x_ref, w_hbm, bg_ref, bl_ref, a_ref, wbuf, w_bf, sem):
    i = pl.program_id(1)
    tn = a_ref.shape[1]
    _expert_weights_step(sched_ref, nseg_ref, w_hbm, wbuf, w_bf, sem,
                         lambda j, part: pl.multiple_of(part * D_FF + j * tn, LANES))
    for r in range(MOE_BLK // MOE_SUB + 1):
        @pl.when(sched_ref[2, i] == r)
        def _():
            used = r * MOE_SUB
            if used:
                x = x_ref[0:used, :]
                hg = jnp.dot(x, w_bf[0], preferred_element_type=F32) + bg_ref[0]
                hl = jnp.dot(x, w_bf[1], preferred_element_type=F32) + bl_ref[0]
                glu = jnp.minimum(hg, SWIGLU_LIMIT)
                lin = jnp.clip(hl, -SWIGLU_LIMIT, SWIGLU_LIMIT)
                a_ref[0:used, :] = (glu * jax.nn.sigmoid(SWIGLU_ALPHA * glu) * (lin + 1.0)).astype(a_ref.dtype)
            if used < MOE_BLK:
                a_ref[used:MOE_BLK, :] = jnp.zeros((MOE_BLK - used, a_ref.shape[1]), a_ref.dtype)


def _gmm_scratch(k, tn, n_parts):
    return [pltpu.VMEM((2, n_parts, k, tn), F32), pltpu.VMEM((n_parts, k, tn), BF16),
            pltpu.SemaphoreType.DMA((2, n_parts))]


def _gmm1(xs, w_gu_l, b_gu_l, sched, nseg):
    n_slots, d = xs.shape
    nb = n_slots // MOE_BLK
    tn = min(512, D_FF)
    nj = D_FF // tn
    b3 = b_gu_l.reshape(N_EXPERTS, 1, 2 * D_FF)
    grid_spec = pltpu.PrefetchScalarGridSpec(
        num_scalar_prefetch=2, grid=(nj, nb),
        in_specs=[pl.BlockSpec((MOE_BLK, d), lambda j, i, sc, ns: (sc[5, i], 0)),
                  pl.BlockSpec(memory_space=pl.ANY),
                  pl.BlockSpec((1, 1, tn), lambda j, i, sc, ns: (sc[0, i], 0, j)),
                  pl.BlockSpec((1, 1, tn), lambda j, i, sc, ns: (sc[0, i], 0, nj + j))],
        out_specs=pl.BlockSpec((MOE_BLK, tn), lambda j, i, sc, ns: (i, j)),
        scratch_shapes=_gmm_scratch(d, tn, 2))
    return pl.pallas_call(
        _gmm1_body, grid_spec=grid_spec,
        out_shape=jax.ShapeDtypeStruct((n_slots, D_FF), BF16),
        compiler_params=_cparams(("arbitrary", "arbitrary")),
        name="moe_up_swiglu",
    )(sched, nseg, xs, w_gu_l, b3, b3)


def _gmm2_body(sched_ref, nseg_ref, a_ref, w_hbm, b_ref, y_ref, wbuf, w_bf, sem):
    i = pl.program_id(1)
    tn = y_ref.shape[1]
    _expert_weights_step(sched_ref, nseg_ref, w_hbm, wbuf, w_bf, sem,
                         lambda j, part: pl.multiple_of(j * tn, LANES))
    for r in range(MOE_BLK // MOE_SUB + 1):
        @pl.when(sched_ref[2, i] == r)
        def _():
            used = r * MOE_SUB
            if used:
                y_ref[0:used, :] = jnp.dot(a_ref[0:used, :], w_bf[0], preferred_element_type=F32) + b_ref[0]
            if used < MOE_BLK:
                y_ref[used:MOE_BLK, :] = jnp.zeros((MOE_BLK - used, y_ref.shape[1]), y_ref.dtype)


def _gmm2(a, w_dn_l, b_dn_l, sched, nseg):
    n_slots, f = a.shape
    nb = n_slots // MOE_BLK
    tn = min(1024, D_MODEL)
    nj = D_MODEL // tn
    b3 = b_dn_l.reshape(N_EXPERTS, 1, D_MODEL)
    grid_spec = pltpu.PrefetchScalarGridSpec(
        num_scalar_prefetch=2, grid=(nj, nb),
        in_specs=[pl.BlockSpec((MOE_BLK, f), lambda j, i, sc, ns: (sc[5, i], 0)),
                  pl.BlockSpec(memory_space=pl.ANY),
                  pl.BlockSpec((1, 1, tn), lambda j, i, sc, ns: (sc[0, i], 0, j))],
        out_specs=pl.BlockSpec((MOE_BLK, tn), lambda j, i, sc, ns: (i, j)),
        scratch_shapes=_gmm_scratch(f, tn, 1))
    return pl.pallas_call(
        _gmm2_body, grid_spec=grid_spec,
        out_shape=jax.ShapeDtypeStruct((n_slots, D_MODEL), F32),
        compiler_params=_cparams(("arbitrary", "arbitrary")),
        name="moe_down",
    )(sched, nseg, a, w_dn_l, b3)


DMA_ISSUE_UNROLL = 8


def _moe_gather_body(nsub_ref, tok_ref, h_hbm, o_ref, buf, sem):
    used = nsub_ref[pl.program_id(0)] * MOE_SUB

    for s in range(MOE_BLK // MOE_SUB):
        def start_row(r, carry, s=s):
            pltpu.make_async_copy(h_hbm.at[pl.ds(tok_ref[0, 0, r], 1)], buf.at[pl.ds(r, 1)], sem.at[s]).start()
            return carry

        @pl.when(s * MOE_SUB < used)
        def _():
            lax.fori_loop(s * MOE_SUB, (s + 1) * MOE_SUB, start_row, 0, unroll=DMA_ISSUE_UNROLL)

    for s in range(MOE_BLK // MOE_SUB):
        rows = slice(s * MOE_SUB, (s + 1) * MOE_SUB)

        @pl.when(s * MOE_SUB < used)
        def _():
            pltpu.make_async_copy(h_hbm.at[pl.ds(0, MOE_SUB)], buf.at[rows], sem.at[s]).wait()
            o_ref[rows, :] = buf[rows, :].astype(o_ref.dtype)

        @pl.when(s * MOE_SUB >= used)
        def _():
            o_ref[rows, :] = jnp.zeros((MOE_SUB, o_ref.shape[1]), o_ref.dtype)


def _moe_gather(h32, slot_tok, nsub):
    d = h32.shape[1]
    nb = nsub.shape[0]
    grid_spec = pltpu.PrefetchScalarGridSpec(
        num_scalar_prefetch=1, grid=(nb,),
        in_specs=[pl.BlockSpec((1, 1, MOE_BLK), lambda i, ns: (i, 0, 0), memory_space=pltpu.SMEM),
                  pl.BlockSpec(memory_space=pl.ANY)],
        out_specs=pl.BlockSpec((MOE_BLK, d), lambda i, ns: (i, 0)),
        scratch_shapes=[pltpu.VMEM((MOE_BLK, d), F32), pltpu.SemaphoreType.DMA((MOE_BLK // MOE_SUB,))])
    return pl.pallas_call(
        _moe_gather_body, grid_spec=grid_spec,
        out_shape=jax.ShapeDtypeStruct((nb * MOE_BLK, d), BF16),
        compiler_params=_cparams(("arbitrary",)),
        name="moe_gather_rows",
    )(nsub, slot_tok.reshape(nb, 1, MOE_BLK), h32)


COMBINE_TILE = 192


def _combine_row_copy(y_hbm, ybuf, sem, slot, t, k):
    return pltpu.make_async_copy(y_hbm.at[pl.ds(slot, 1)], ybuf.at[k, pl.ds(t, 1)], sem.at[k])


def _combine_ln_body(dest_ref, y_hbm, gate_ref, h_ref, g_ref, b_ref, o_ref, ybuf, sem, *, alpha):
    tt = h_ref.shape[0]

    def start_rows(t, carry):
        for k in range(TOP_K):
            _combine_row_copy(y_hbm, ybuf, sem, dest_ref[0, 0, t * TOP_K + k], t, k).start()
        return carry

    lax.fori_loop(0, tt, start_rows, 0, unroll=DMA_ISSUE_UNROLL)
    for k in range(TOP_K):
        pltpu.make_async_copy(y_hbm.at[pl.ds(0, tt)], ybuf.at[k], sem.at[k]).wait()
    gate = gate_ref[...]
    f = gate[:, 0:1] * ybuf[0]
    for k in range(1, TOP_K):
        f = f + gate[:, k:k + 1] * ybuf[k]
    v = alpha * h_ref[...] + f
    mu = jnp.mean(v, axis=-1, keepdims=True)
    var = jnp.mean(jnp.square(v - mu), axis=-1, keepdims=True)
    o_ref[...] = (v - mu) * lax.rsqrt(var + LN_EPS) * g_ref[...] + b_ref[...]


def _combine_ln(h32, y, dest, gate_pad, g, b, alpha):
    rows, d = h32.shape
    tt = COMBINE_TILE
    nt = rows // tt
    row = pl.BlockSpec((tt, d), lambda i: (i, 0))
    vec = pl.BlockSpec((1, d), lambda i: (0, 0))
    return pl.pallas_call(
        functools.partial(_combine_ln_body, alpha=alpha), grid=(nt,),
        in_specs=[pl.BlockSpec((1, 1, tt * TOP_K), lambda i: (i, 0, 0), memory_space=pltpu.SMEM),
                  pl.BlockSpec(memory_space=pl.ANY),
                  pl.BlockSpec((tt, LANES), lambda i: (i, 0)), row, vec, vec],
        out_specs=row,
        out_shape=jax.ShapeDtypeStruct((rows, d), F32),
        scratch_shapes=[pltpu.VMEM((TOP_K, tt, d), F32), pltpu.SemaphoreType.DMA((TOP_K,))],
        compiler_params=_cparams(("arbitrary",)),
        name="moe_combine_ln2",
    )(dest.reshape(nt, 1, tt * TOP_K), y, gate_pad, h32, g.reshape(1, d), b.reshape(1, d))


def _moe(h32, n_tok, router_w_l, router_b_l, w_gu_l, b_gu_l, w_dn_l, b_dn_l):
    rows = h32.shape[0]
    idx_pad, gate_pad = _router(h32, router_w_l, router_b_l)
    top_idx = idx_pad[:n_tok, :TOP_K]
    gate = gate_pad[:n_tok, :TOP_K]
    n_assign = n_tok * TOP_K
    n_blocks = -(-n_assign // MOE_BLK) + N_EXPERTS
    n_slots = n_blocks * MOE_BLK
    flat_e = top_idx.reshape(-1)
    onehot = (flat_e[:, None] == jnp.arange(N_EXPERTS, dtype=jnp.int32)[None, :]).astype(jnp.int32)
    counts = jnp.sum(onehot, axis=0)
    padded = (counts + MOE_BLK - 1) // MOE_BLK * MOE_BLK
    pad_end = jnp.cumsum(padded)
    pad_start = pad_end - padded
    dest = jnp.sum((jnp.cumsum(onehot, axis=0) - onehot + pad_start[None, :]) * onehot, axis=1)
    slot_tok = jnp.zeros((n_slots,), jnp.int32).at[dest].set(jnp.arange(n_assign, dtype=jnp.int32) // TOP_K)
    n_used = pad_end[-1] // MOE_BLK
    blk = jnp.minimum(jnp.arange(n_blocks, dtype=jnp.int32), n_used - 1)
    block_e = jnp.minimum(jnp.sum((pad_end[None, :] <= (blk * MOE_BLK)[:, None]).astype(jnp.int32), axis=1),
                          N_EXPERTS - 1)
    first = jnp.concatenate([jnp.ones((1,), jnp.int32), (block_e[1:] != block_e[:-1]).astype(jnp.int32)])
    seg_rows = jnp.clip(counts[block_e] - (blk * MOE_BLK - pad_start[block_e]), 0, MOE_BLK)
    nsub = jnp.where(jnp.arange(n_blocks) < n_used, (seg_rows + MOE_SUB - 1) // MOE_SUB, 0).astype(jnp.int32)
    seg_idx = jnp.cumsum(first) - 1
    nseg = seg_idx[-1] + 1
    seg_expert = jnp.zeros((n_blocks + 1,), jnp.int32).at[jnp.where(first == 1, seg_idx, n_blocks)].set(block_e)
    next_expert = jnp.where(seg_idx + 1 < nseg, seg_expert[jnp.minimum(seg_idx + 1, n_blocks - 1)], block_e[0])
    sched = jnp.stack([block_e, first, nsub, seg_idx, next_expert, blk]).astype(jnp.int32)
    xs = _moe_gather(h32, slot_tok, nsub)
    a = _gmm1(xs, w_gu_l, b_gu_l, sched, nseg.reshape(1).astype(jnp.int32))
    y = _gmm2(a, w_dn_l, b_dn_l, sched, nseg.reshape(1).astype(jnp.int32))
    dest_pad = jnp.pad(dest.reshape(n_tok, TOP_K), ((0, rows - n_tok), (0, 0)))
    row_is_token = (jnp.arange(rows) < n_tok)[:, None]
    return y, dest_pad, jnp.where(row_is_token, gate_pad, 0.0)


def _attention_bias_tables(rel_bias):
    qi = jnp.arange(Q_BLOCK)[:, None]
    kj = jnp.arange(2 * Q_BLOCK)[None, :]
    rel = qi + Q_BLOCK - kj
    prompt_tabs, cache_tabs, new_tabs = [], [], []
    buckets = jnp.arange(NUM_BUCKETS, dtype=jnp.int32)
    for gi, (win, dil) in enumerate(GROUPS):
        steps = win // dil
        bias_g = rel_bias[:, gi * H_B:(gi + 1) * H_B].astype(F32)
        hit = _t5_bucket(jnp.maximum(rel, 0) * dil)[None, :, :, None] == buckets
        tab = jnp.sum(jnp.where(hit, bias_g.T[:, None, None, :], 0.0), axis=-1)
        prompt_tabs.append(jnp.where(((rel >= 0) & (rel <= steps))[None], tab, -jnp.inf))
        bj = bias_g[_t5_bucket(jnp.arange(steps + 1) * dil)]
        cache_tabs.append(bj[:0:-1, :, None])
        new_tabs.append(bj[0, :, None])
    return prompt_tabs, jnp.stack(cache_tabs), jnp.stack(new_tabs)


def kernel(x_prompt, x_sample, cache_kv_w128, cache_kv_w512, cache_kv_w2048, state_conv, state_delta, rel_bias,
           w_in, b_gate, conv_w, a_log, dt_bias, o_norm_w, w_branch_a, w_branch_b, w_out,
           ln1_g, ln1_b, router_w, router_b, w_gu, b_gu, w_dn, b_dn, ln2_g, ln2_b):
    batch, seq, d = x_prompt.shape
    n_s = x_sample.shape[0]
    n_p = batch * seq
    n_tok = n_p + n_s
    rows = 8 * ROW_TILE
    depth = w_in.shape[0]
    alpha = (2.0 * depth) ** 0.25
    kv_caches = (cache_kv_w128, cache_kv_w512, cache_kv_w2048)
    hw = H_B * DH_B

    x = jnp.concatenate([x_prompt.reshape(n_p, d), x_sample.reshape(n_s, d),
                         jnp.zeros((rows - n_tok, d), F32)], axis=0)
    prompt_tabs, bias_cache, bias_new = _attention_bias_tables(rel_bias)

    kv_p = [[] for _ in GROUPS]
    kv_s = [[] for _ in GROUPS]
    conv_p, conv_s, delta_p, delta_s = [], [], [], []
    for l in range(depth):
        xb = x.astype(BF16)
        w_in_t = jnp.swapaxes(w_in, 1, 2)
        in_proj = functools.partial(_mm, xb, w_in_t, layer=l, tm=ROW_TILE, w_is_nk=True)
        p_a = in_proj(n_cols=OFF_BETA, tn=512, name="in_proj_deltanet")
        p_ba = in_proj(n_cols=LANES, col0=OFF_BETA, tn=LANES, name="in_proj_beta_alpha")
        p_b = in_proj(n_cols=N_GROUPS * QKV_B, col0=OFF_B, tn=512, name="in_proj_attention")
        gates = in_proj(n_cols=2 * D_MODEL, col0=OFF_GATE, tn=512,
                        bias=b_gate[l].reshape(1, 2 * D_MODEL), name="in_proj_gates")

        beta, g, gc = _beta_g(p_ba, a_log[l], dt_bias[l])
        qkv = _conv_qkv_prompt(p_a, conv_w[l], n_p, seq)
        ya_p, s_p = _gdn_prompt(qkv, p_a, beta[:n_p], gc[:n_p], o_norm_w[l], batch, seq)
        ya_s, cs, s_s = _gdn_sample(p_a[n_p:n_tok].reshape(n_s, 1, OFF_BETA), state_conv[l], conv_w[l],
                                    beta[n_p:n_tok], g[n_p:n_tok], state_delta[l], o_norm_w[l])
        ya = jnp.concatenate([ya_p, ya_s.reshape(n_s, H_A * DV_A),
                              jnp.zeros((rows - n_tok, H_A * DV_A), BF16)], axis=0)

        outs, lses = [], []
        for gi in range(N_GROUPS):
            o_g, lse_g = _attn_prompt(p_b, gi, prompt_tabs[gi], batch, seq)
            outs.append(o_g)
            lses.append(lse_g)
        yb_p = _merge_groups(outs, lses)
        p_b_s = p_b[n_p:n_tok]
        yb_s = _attn_sample(p_b_s.reshape(n_s, 3 * N_GROUPS, H_B, DH_B), [c[l] for c in kv_caches],
                            bias_cache, bias_new)
        yb = jnp.concatenate([yb_p, yb_s.reshape(n_s, hw), jnp.zeros((rows - n_tok, hw), BF16)], axis=0)

        merged = _branch_merge(ya, yb, w_branch_a[l], w_branch_b[l], gates)
        mix = _mm(merged, w_out, layer=l, n_cols=D_MODEL, tm=ROW_TILE, tn=512, name="out_proj")
        (h32,) = _residual_ln(x, mix, ln1_g[l], ln1_b[l], alpha, (F32,), "deepnorm_ln1")
        y_slots, dest, gate = _moe(h32, n_tok, router_w[l], router_b[l], w_gu[l], b_gu[l], w_dn[l], b_dn[l])
        x = _combine_ln(h32, y_slots, dest, gate, ln2_g[l], ln2_b[l], alpha)

        new_rows = []
        for gi, (win, dil) in enumerate(GROUPS):
            keep = min(win, seq)
            c0, c1 = gi * QKV_B + hw, (gi + 1) * QKV_B
            last = [p_b[(b + 1) * seq - keep:(b + 1) * seq, c0:c1] for b in range(batch)]
            kv_p[gi].append(jnp.stack(last).reshape(batch, keep, 2, H_B, DH_B))
            new_rows.append(p_b_s[:, c0:c1].reshape(n_s, 1, 2, H_B, DH_B))
        for gi, shifted in enumerate(_cache_shift([c[l] for c in kv_caches], new_rows)):
            kv_s[gi].append(shifted)
        conv_p.append(jnp.stack([p_a[(b + 1) * seq - (CONV_W - 1):(b + 1) * seq, :CONV_DIM] for b in range(batch)]))
        conv_s.append(cs)
        delta_p.append(s_p)
        delta_s.append(s_s)

    return (x[:n_p].reshape(batch, seq, d), x[n_p:n_tok].reshape(n_s, 1, d),
            jnp.stack(kv_p[0]), jnp.stack(kv_s[0]),
            jnp.stack(kv_p[1]), jnp.stack(kv_s[1]),
            jnp.stack(kv_p[2]), jnp.stack(kv_s[2]),
            jnp.stack(conv_p), jnp.stack(conv_s),
            jnp.stack(delta_p), jnp.stack(delta_s))
```

```python
import functools
import math

import jax
import jax.numpy as jnp
from jax import lax
from jax.experimental import pallas as pl
from jax.experimental.pallas import tpu as pltpu

F32 = jnp.float32
BF16 = jnp.bfloat16

D_MODEL = 4096
H_A, DK_A, DV_A = 16, 128, 128
CONV_W = 4
CONV_DIM = H_A * (2 * DK_A + DV_A)
CHUNK = 64
H_B, DH_B = 8, 128
GROUPS = ((128, 1), (512, 4), (2048, 16))
N_GROUPS = len(GROUPS)
Q_BLOCK = 128
NUM_BUCKETS, MAX_DISTANCE = 32, 2048
N_EXPERTS, TOP_K = 32, 4
D_FF = D_MODEL
SWIGLU_LIMIT, SWIGLU_ALPHA = 7.0, 1.702
LN_EPS, RMS_EPS = 1e-5, 1e-6

OFF_Z = CONV_DIM
OFF_BETA = OFF_Z + H_A * DV_A
OFF_ALPHA = OFF_BETA + H_A
OFF_B = OFF_ALPHA + H_A
QKV_B = 3 * H_B * DH_B
OFF_GATE = OFF_B + N_GROUPS * QKV_B

LANES = 128
VMEM_LIMIT = 56 * 1024 * 1024
ROW_TILE = 1032
MOE_BLK = 512
MOE_SUB = 128


def _cparams(sem):
    return pltpu.CompilerParams(dimension_semantics=sem, vmem_limit_bytes=VMEM_LIMIT)


def _dot(a, b):
    return jnp.dot(a.astype(BF16), b.astype(BF16), preferred_element_type=F32)


def _dot_nt(a, b):
    return lax.dot_general(a.astype(BF16), b.astype(BF16), (((1,), (1,)), ((), ())),
                           preferred_element_type=F32)


def _dot_tn(a, b):
    return lax.dot_general(a.astype(BF16), b.astype(BF16), (((0,), (0,)), ((), ())),
                           preferred_element_type=F32)


def _dot_hi(a, b):
    return jnp.dot(a, b, precision=lax.Precision.HIGHEST, preferred_element_type=F32)


def _silu(x):
    return x * jax.nn.sigmoid(x)


def _rb(x):
    return x.astype(BF16).astype(F32)


def _mm_body(*refs, sigmoid_bias, w_is_nk):
    if sigmoid_bias:
        x_ref, w_ref, b_ref, o_ref = refs
    else:
        x_ref, w_ref, o_ref = refs
    w = w_ref[0].astype(BF16)
    contract = (((1,), (1,)), ((), ())) if w_is_nk else (((1,), (0,)), ((), ()))
    acc = lax.dot_general(x_ref[...], w, contract, preferred_element_type=F32)
    if sigmoid_bias:
        acc = jax.nn.sigmoid(acc + b_ref[...])
    o_ref[...] = acc.astype(o_ref.dtype)


def _mm(x, w, *, layer=0, n_cols, col0=0, tm, tn, w_is_nk=False, bias=None, out_dtype=F32, name):
    m, k = x.shape
    if w_is_nk:
        w_spec = pl.BlockSpec((pl.Element(1), pl.Element(tn), pl.Element(k)),
                              lambda i, j: (layer, pl.multiple_of(col0 + j * tn, 8), 0))
    else:
        w_spec = pl.BlockSpec((1, k, tn), lambda i, j: (layer, 0, j + col0 // tn))
    in_specs = [pl.BlockSpec((tm, k), lambda i, j: (i, 0)), w_spec]
    args = [x, w]
    if bias is not None:
        in_specs.append(pl.BlockSpec((1, tn), lambda i, j: (0, j)))
        args.append(bias)
    return pl.pallas_call(
        functools.partial(_mm_body, sigmoid_bias=bias is not None, w_is_nk=w_is_nk),
        grid=(m // tm, n_cols // tn),
        in_specs=in_specs,
        out_specs=pl.BlockSpec((tm, tn), lambda i, j: (i, j)),
        out_shape=jax.ShapeDtypeStruct((m, n_cols), out_dtype),
        compiler_params=_cparams(("parallel", "parallel")),
        name=name,
    )(*args)


def _bg_body(p_ref, a_ref, dt_ref, beta_ref, g_ref, gc_ref):
    p = p_ref[...]
    tt = p.shape[0]
    beta_ref[...] = jax.nn.sigmoid(p)
    g = -jnp.exp(a_ref[...]) * jax.nn.softplus(p + dt_ref[...])
    g_ref[...] = g
    ri = lax.broadcasted_iota(jnp.int32, (tt, tt), 0)
    ci = lax.broadcasted_iota(jnp.int32, (tt, tt), 1)
    same_chunk_before = ((ri // CHUNK) == (ci // CHUNK)) & (ci <= ri)
    gc_ref[...] = _dot_hi(same_chunk_before.astype(F32), g)


def _beta_g(p_ba, a_log_l, dt_bias_l):
    rows = p_ba.shape[0]
    tt = 3 * CHUNK
    zeros = jnp.zeros((LANES - 2 * H_A,), F32)
    a_vec = jnp.concatenate([jnp.zeros((H_A,), F32), a_log_l, zeros]).reshape(1, LANES)
    dt_vec = jnp.concatenate([jnp.zeros((H_A,), F32), dt_bias_l, zeros]).reshape(1, LANES)
    row_spec = pl.BlockSpec((tt, LANES), lambda i: (i, 0))
    vec_spec = pl.BlockSpec((1, LANES), lambda i: (0, 0))
    shp = jax.ShapeDtypeStruct((rows, LANES), F32)
    beta, g, gc = pl.pallas_call(
        _bg_body, grid=(rows // tt,),
        in_specs=[row_spec, vec_spec, vec_spec],
        out_specs=[row_spec, row_spec, row_spec],
        out_shape=[shp, shp, shp],
        compiler_params=_cparams(("parallel",)),
        name="deltanet_gates",
    )(p_ba, a_vec, dt_vec)
    return beta[:, :H_A], g[:, H_A:2 * H_A], gc[:, H_A:2 * H_A]


def _conv_body(cur_ref, halo_ref, w_ref, o_ref, *, seq, tt):
    i = pl.program_id(0)
    j = pl.program_id(1)
    cur = cur_ref[...]
    halo = jnp.where((i * tt) % seq == 0, 0.0, halo_ref[...])
    xh = jnp.concatenate([halo, cur], axis=0)
    w = w_ref[...]
    acc = xh[5:5 + tt] * w[0:1]
    for t in range(1, CONV_W):
        acc = acc + xh[5 + t:5 + t + tt] * w[t:t + 1]
    y = _silu(acc)
    ct = y.shape[1]
    n_qk_blocks = 2 * H_A * DK_A // ct
    is_q = j < H_A * DK_A // ct
    is_qk = j < n_qk_blocks
    scale = jnp.where(is_q, DK_A ** -0.5, 1.0)
    for h in range(ct // DK_A):
        sl = slice(h * DK_A, (h + 1) * DK_A)
        yh = y[:, sl]
        nrm = yh * lax.rsqrt(jnp.sum(yh * yh, axis=-1, keepdims=True) + RMS_EPS) * scale
        o_ref[:, sl] = jnp.where(is_qk, nrm, yh)


def _conv_qkv_prompt(p_a, conv_w_l, n_rows, seq):
    tt, ct = 256, 1024
    return pl.pallas_call(
        functools.partial(_conv_body, seq=seq, tt=tt),
        grid=(n_rows // tt, CONV_DIM // ct),
        in_specs=[pl.BlockSpec((tt, ct), lambda i, j: (i, j)),
                  pl.BlockSpec((8, ct), lambda i, j: (jnp.maximum(i * (tt // 8) - 1, 0), j)),
                  pl.BlockSpec((CONV_W, ct), lambda i, j: (0, j))],
        out_specs=pl.BlockSpec((tt, ct), lambda i, j: (i, j)),
        out_shape=jax.ShapeDtypeStruct((n_rows, CONV_DIM), F32),
        compiler_params=_cparams(("parallel", "parallel")),
        name="conv_silu_l2norm",
    )(p_a, p_a, conv_w_l)


def _split_bf16(a):
    hi = a.astype(BF16)
    return hi, (a - hi.astype(F32)).astype(BF16)


def _dot_3pass(a, b):
    a_hi, a_lo = _split_bf16(a)
    b_hi, b_lo = _split_bf16(b)
    return (jnp.dot(a_hi, b_hi, preferred_element_type=F32) + jnp.dot(a_hi, b_lo, preferred_element_type=F32)
            + jnp.dot(a_lo, b_hi, preferred_element_type=F32))


def _gdn_wy_body(q_ref, k_ref, v_ref, b_ref, gc_ref, gr_ref, u_ref, w_ref, qg_ref, kd_ref, attn_ref, *, hb):
    C = CHUNK
    ri = lax.broadcasted_iota(jnp.int32, (C, C), 0)
    ci = lax.broadcasted_iota(jnp.int32, (C, C), 1)
    tril = ci <= ri
    strict = ci < ri
    ys, zs = [], []
    for hh in range(hb):
        sl = slice(hh * DK_A, (hh + 1) * DK_A)
        q = q_ref[:, sl]
        k = k_ref[:, sl]
        v = v_ref[:, sl]
        b = b_ref[0, 0, 0][:, hh:hh + 1]
        gc = gc_ref[0, 0, 0][:, hh:hh + 1]
        gr = gr_ref[0, 0, 0][hh:hh + 1, :]
        decay = jnp.where(tril, jnp.exp(jnp.where(tril, gc - gr, 0.0)), 0.0)
        kb = k * b
        eg = jnp.exp(gc)
        ys.append(-jnp.where(strict, _dot_nt(kb, k) * decay, 0.0))
        zs.append(jnp.concatenate([v * b, kb * eg], axis=-1))
        qg_ref[:, sl] = (q * eg).astype(qg_ref.dtype)
        kd_ref[:, sl] = (k * jnp.exp(gc[C - 1:C, :] - gc)).astype(kd_ref.dtype)
        attn_ref[0, hh] = jnp.where(tril, _dot_nt(q, k) * decay, 0.0).astype(attn_ref.dtype)
    for step in range(6):
        for hh in range(hb):
            y, z = ys[hh], zs[hh]
            if step < 5:
                prod = _dot_3pass(y, jnp.concatenate([z, y], axis=-1))
                zs[hh] = z + prod[:, :DV_A + DK_A]
                ys[hh] = prod[:, DV_A + DK_A:]
            else:
                zs[hh] = z + _dot_3pass(y, z)
    for hh in range(hb):
        sl = slice(hh * DK_A, (hh + 1) * DK_A)
        u_ref[:, sl] = zs[hh][:, :DV_A]
        w_ref[:, sl] = zs[hh][:, DV_A:].astype(w_ref.dtype)


def _gdn_scan_body(u_ref, w_ref, qg_ref, kd_ref, attn_ref, gl_ref, z_ref, nw_ref, ya_ref, s_ref):
    c = pl.program_id(1)

    @pl.when(c == 0)
    def _():
        s_ref[...] = jnp.zeros_like(s_ref)

    nw = nw_ref[...]
    v_news, o_states = [], []
    for hh in range(H_A):
        sl = slice(hh * DK_A, (hh + 1) * DK_A)
        s_bf = s_ref[0, hh].astype(BF16)
        v_news.append((u_ref[:, sl] - jnp.dot(w_ref[:, sl], s_bf, preferred_element_type=F32)).astype(BF16))
        o_states.append(jnp.dot(qg_ref[:, sl], s_bf, preferred_element_type=F32))
    for hh in range(H_A):
        sl = slice(hh * DK_A, (hh + 1) * DK_A)
        s = s_ref[0, hh]
        v_bf = v_news[hh]
        o = o_states[hh] + jnp.dot(attn_ref[0, hh], v_bf, preferred_element_type=F32)
        decay_last = jnp.exp(gl_ref[0, 0, hh:hh + 1, :])
        s_ref[0, hh] = s * decay_last + lax.dot_general(kd_ref[:, sl], v_bf, (((0,), (0,)), ((), ())),
                                                         preferred_element_type=F32)
        o = o * lax.rsqrt(jnp.mean(o * o, axis=-1, keepdims=True) + RMS_EPS) * nw
        ya_ref[:, sl] = (o * _silu(z_ref[:, sl])).astype(ya_ref.dtype)


def _gdn_prompt(qkv, p_a, beta, gc, o_norm_w_l, batch, seq):
    hb = 8
    nhb = H_A // hb
    nc = seq // CHUNK
    w = hb * DK_A
    hw = H_A * DK_A
    n = batch * seq

    def per_head_cols(a):
        return a.reshape(batch, nc, CHUNK, nhb, hb).transpose(0, 3, 1, 2, 4)

    beta_c = per_head_cols(beta)
    gc_c = per_head_cols(gc)
    gc_r = gc_c.transpose(0, 1, 2, 4, 3)
    col_spec = pl.BlockSpec((1, 1, 1, CHUNK, hb), lambda b, h, c: (b, h, c, 0, 0))
    row_spec = pl.BlockSpec((1, 1, 1, hb, CHUNK), lambda b, h, c: (b, h, c, 0, 0))

    def tok_spec(col_block0):
        return pl.BlockSpec((CHUNK, w), lambda b, h, c: (b * nc + c, col_block0 + h))

    attn_spec = pl.BlockSpec((1, hb, CHUNK, CHUNK), lambda b, h, c: (b * nc + c, h, 0, 0))
    u, wf, qg, kd, attn = pl.pallas_call(
        functools.partial(_gdn_wy_body, hb=hb),
        grid=(batch, nhb, nc),
        in_specs=[tok_spec(0), tok_spec(hw // w), tok_spec(2 * hw // w), col_spec, col_spec, row_spec],
        out_specs=[tok_spec(0), tok_spec(0), tok_spec(0), tok_spec(0), attn_spec],
        out_shape=[jax.ShapeDtypeStruct((n, hw), F32), jax.ShapeDtypeStruct((n, hw), BF16),
                   jax.ShapeDtypeStruct((n, hw), BF16), jax.ShapeDtypeStruct((n, hw), BF16),
                   jax.ShapeDtypeStruct((batch * nc, H_A, CHUNK, CHUNK), BF16)],
        compiler_params=_cparams(("parallel", "parallel", "parallel")),
        name="gated_delta_wy",
    )(qkv, qkv, qkv, beta_c, gc_c, gc_r)

    g_last = jnp.broadcast_to(gc.reshape(batch, nc, CHUNK, H_A)[:, :, CHUNK - 1, :, None], (batch, nc, H_A, LANES))
    row = pl.BlockSpec((CHUNK, hw), lambda b, c: (b * nc + c, 0))
    ya, s_fin = pl.pallas_call(
        _gdn_scan_body,
        grid=(batch, nc),
        in_specs=[row, row, row, row,
                  pl.BlockSpec((1, H_A, CHUNK, CHUNK), lambda b, c: (b * nc + c, 0, 0, 0)),
                  pl.BlockSpec((1, 1, H_A, LANES), lambda b, c: (b, c, 0, 0)),
                  pl.BlockSpec((CHUNK, hw), lambda b, c: (b * nc + c, OFF_Z // hw)),
                  pl.BlockSpec((1, DV_A), lambda b, c: (0, 0))],
        out_specs=[row, pl.BlockSpec((1, H_A, DK_A, DV_A), lambda b, c: (b, 0, 0, 0))],
        out_shape=[jax.ShapeDtypeStruct((n, hw), BF16),
                   jax.ShapeDtypeStruct((batch, H_A, DK_A, DV_A), F32)],
        compiler_params=_cparams(("parallel", "arbitrary")),
        name="gated_delta_scan",
    )(u, wf, qg, kd, attn, g_last, p_a, o_norm_w_l.reshape(1, DV_A))
    return ya, s_fin


def _gdn_sample_body(pa_ref, cbuf_ref, cw_ref, bg_ref, s_ref, nw_ref, ya_ref, conv_ref, so_ref):
    new = pa_ref[0, :, :CONV_DIM]
    buf = cbuf_ref[0]
    w = cw_ref[...]
    acc = buf[0:1] * w[0:1]
    for t in range(1, CONV_W - 1):
        acc = acc + buf[t:t + 1] * w[t:t + 1]
    y = _silu(acc + new * w[CONV_W - 1:CONV_W])
    conv_ref[0, 0:2, :] = buf[1:3]
    conv_ref[0, 2:3, :] = new
    z = pa_ref[0, :, OFF_Z:OFF_BETA]
    bg = bg_ref[0]
    nw = nw_ref[...]
    ri = lax.broadcasted_iota(jnp.int32, (DK_A, DK_A), 0)
    ci = lax.broadcasted_iota(jnp.int32, (DK_A, DK_A), 1)
    eye = ri == ci

    def as_col(row):
        return jnp.sum(jnp.where(eye, row, 0.0), axis=1, keepdims=True)

    for h in range(H_A):
        qh = y[:, h * DK_A:(h + 1) * DK_A]
        kh = y[:, (H_A + h) * DK_A:(H_A + h + 1) * DK_A]
        vh = y[:, (2 * H_A + h) * DK_A:(2 * H_A + h + 1) * DK_A]
        qh = qh * lax.rsqrt(jnp.sum(qh * qh, axis=-1, keepdims=True) + RMS_EPS) * (DK_A ** -0.5)
        kh = kh * lax.rsqrt(jnp.sum(kh * kh, axis=-1, keepdims=True) + RMS_EPS)
        beta = bg[0:1, h:h + 1]
        g = bg[1:2, h:h + 1]
        k_col = as_col(kh)
        s = s_ref[0, h] * jnp.exp(g)
        kv = jnp.sum(_rb(s) * _rb(k_col), axis=0, keepdims=True)
        s = s + k_col * ((vh - kv) * beta)
        so_ref[0, h] = s
        o = jnp.sum(_rb(s) * _rb(as_col(qh)), axis=0, keepdims=True)
        o = o * lax.rsqrt(jnp.mean(o * o, axis=-1, keepdims=True) + RMS_EPS) * nw
        ya_ref[0, :, h * DV_A:(h + 1) * DV_A] = (o * _silu(z[:, h * DV_A:(h + 1) * DV_A])).astype(ya_ref.dtype)


def _gdn_sample(p_a_s, conv_buf, conv_w_l, beta_s, g_s, s_delta, o_norm_w_l):
    n = p_a_s.shape[0]
    bg = jnp.stack([beta_s, g_s], axis=1)
    return pl.pallas_call(
        _gdn_sample_body, grid=(n,),
        in_specs=[pl.BlockSpec((1, 1, OFF_BETA), lambda i: (i, 0, 0)),
                  pl.BlockSpec((1, CONV_W - 1, CONV_DIM), lambda i: (i, 0, 0)),
                  pl.BlockSpec((CONV_W, CONV_DIM), lambda i: (0, 0)),
                  pl.BlockSpec((1, 2, H_A), lambda i: (i, 0, 0)),
                  pl.BlockSpec((1, H_A, DK_A, DV_A), lambda i: (i, 0, 0, 0)),
                  pl.BlockSpec((1, DV_A), lambda i: (0, 0))],
        out_specs=[pl.BlockSpec((1, 1, H_A * DV_A), lambda i: (i, 0, 0)),
                   pl.BlockSpec((1, CONV_W - 1, CONV_DIM), lambda i: (i, 0, 0)),
                   pl.BlockSpec((1, H_A, DK_A, DV_A), lambda i: (i, 0, 0, 0))],
        out_shape=[jax.ShapeDtypeStruct((n, 1, H_A * DV_A), BF16),
                   jax.ShapeDtypeStruct((n, CONV_W - 1, CONV_DIM), F32),
                   jax.ShapeDtypeStruct((n, H_A, DK_A, DV_A), F32)],
        compiler_params=_cparams(("parallel",)),
        name="gated_delta_step",
    )(p_a_s, conv_buf, conv_w_l, bg, s_delta, o_norm_w_l.reshape(1, DV_A))


def _t5_bucket(dist):
    max_exact = NUM_BUCKETS // 2
    d = jnp.maximum(dist, 1).astype(F32)
    large = max_exact + (jnp.log(d / max_exact) / math.log(MAX_DISTANCE / max_exact)
                         * (NUM_BUCKETS - max_exact)).astype(jnp.int32)
    large = jnp.minimum(large, NUM_BUCKETS - 1)
    return jnp.where(dist < max_exact, dist, large)


def _attn_prompt_body(q_ref, kp_ref, kc_ref, vp_ref, vc_ref, bias_ref, o_ref, lse_ref, *, dil, hc):
    n = pl.program_id(1)
    hg = pl.program_id(2)
    kj = lax.broadcasted_iota(jnp.int32, (Q_BLOCK, 2 * Q_BLOCK), 1)
    in_seq = (n > 0) | (kj >= Q_BLOCK)
    units = [(pl.ds(r, Q_BLOCK, stride=dil) if dil > 1 else slice(None), h)
             for r in range(dil) for h in range(hc)]
    scores = []
    for rows, h in units:
        sl = slice(h * DH_B, (h + 1) * DH_B)
        kk = jnp.concatenate([kp_ref[rows, sl], kc_ref[rows, sl]], axis=0)
        scores.append(_dot_nt(q_ref[rows, sl], kk))
    for (rows, h), sc in zip(units, scores):
        sl = slice(h * DH_B, (h + 1) * DH_B)
        vv = jnp.concatenate([vp_ref[rows, sl], vc_ref[rows, sl]], axis=0)
        logits = jnp.where(in_seq, sc * (DH_B ** -0.5) + bias_ref[hg * hc + h], -jnp.inf)
        mx = jnp.max(logits, axis=-1, keepdims=True)
        e = jnp.exp(logits - mx)
        den = jnp.sum(e, axis=-1, keepdims=True)
        o_ref[rows, sl] = _dot(e / den, vv)
        lse_ref[rows, sl] = jnp.broadcast_to(mx + jnp.log(den), (Q_BLOCK, DH_B))


def _attn_prompt(p_b, gi, bias_tab, batch, seq):
    _, dil = GROUPS[gi]
    rb = Q_BLOCK * dil
    nb = seq // rb
    hc = H_B if dil == 1 else 1
    cw = hc * DH_B
    hw = H_B * DH_B

    def spec(which, prev):
        def imap(b, n, h):
            nn = jnp.maximum(n - 1, 0) if prev else n
            return (b * nb + nn, (gi * QKV_B + which * hw) // cw + h)
        return pl.BlockSpec((rb, cw), imap)

    out_spec = pl.BlockSpec((rb, cw), lambda b, n, h: (b * nb + n, h))
    shp = jax.ShapeDtypeStruct((batch * seq, hw), F32)
    return pl.pallas_call(
        functools.partial(_attn_prompt_body, dil=dil, hc=hc), grid=(batch, nb, H_B // hc),
        in_specs=[spec(0, False), spec(1, True), spec(1, False), spec(2, True), spec(2, False),
                  pl.BlockSpec((H_B, Q_BLOCK, 2 * Q_BLOCK), lambda b, n, h: (0, 0, 0))],
        out_specs=[out_spec, out_spec],
        out_shape=[shp, shp],
        compiler_params=_cparams(("parallel", "parallel", "parallel")),
        name=f"dilated_attn_prompt_g{gi}",
    )(p_b, p_b, p_b, p_b, p_b, bias_tab)


def _merge_body(o0, o1, o2, l0, l1, l2, yb_ref):
    ls = [l0[...], l1[...], l2[...]]
    mx = jnp.maximum(jnp.maximum(ls[0], ls[1]), ls[2])
    es = [jnp.exp(l - mx) for l in ls]
    den = es[0] + es[1] + es[2]
    acc = (_rb(es[0] / den) * _rb(o0[...]) + _rb(es[1] / den) * _rb(o1[...])
           + _rb(es[2] / den) * _rb(o2[...]))
    yb_ref[...] = acc.astype(yb_ref.dtype)


def _merge_groups(outs, lses):
    rows, hw = outs[0].shape
    tt = 512
    spec = pl.BlockSpec((tt, hw), lambda i: (i, 0))
    return pl.pallas_call(
        _merge_body, grid=(rows // tt,),
        in_specs=[spec] * 6, out_specs=spec,
        out_shape=jax.ShapeDtypeStruct((rows, hw), BF16),
        compiler_params=_cparams(("parallel",)),
        name="merge_attention_groups",
    )(*outs, *lses)


def _attn_sample_body(pb_ref, c0_ref, c1_ref, c2_ref, bc_ref, bn_ref, yb_ref):
    caches = (c0_ref, c1_ref, c2_ref)
    scale = DH_B ** -0.5
    outs, lses = [], []
    for gi in range(N_GROUPS):
        q = pb_ref[0, 3 * gi]
        k_new = pb_ref[0, 3 * gi + 1]
        v_new = pb_ref[0, 3 * gi + 2]
        kc = caches[gi][0, :, 0, 0]
        vc = caches[gi][0, :, 0, 1]
        qr = _rb(q)
        lc = jnp.sum(_rb(kc) * qr[None], axis=-1, keepdims=True) * scale + bc_ref[gi]
        ln = jnp.sum(_rb(k_new) * qr, axis=-1, keepdims=True) * scale + bn_ref[gi]
        mx = jnp.maximum(jnp.max(lc, axis=0), ln)
        ec = jnp.exp(lc - mx[None])
        en = jnp.exp(ln - mx)
        den = jnp.sum(ec, axis=0) + en
        outs.append(jnp.sum(_rb(ec / den[None]) * _rb(vc), axis=0) + _rb(en / den) * _rb(v_new))
        lses.append(mx + jnp.log(den))
    mx = jnp.maximum(jnp.maximum(lses[0], lses[1]), lses[2])
    es = [jnp.exp(l - mx) for l in lses]
    den = es[0] + es[1] + es[2]
    acc = _rb(es[0] / den) * _rb(outs[0]) + _rb(es[1] / den) * _rb(outs[1]) + _rb(es[2] / den) * _rb(outs[2])
    yb_ref[0] = acc.astype(yb_ref.dtype)


def _attn_sample(p_b_s, caches, bias_cache, bias_new):
    n = p_b_s.shape[0]
    views, specs = [], []
    for (win, dil), c in zip(GROUPS, caches):
        wb = c.shape[1]
        views.append(c.reshape(n, wb // dil, dil, 2, H_B, DH_B))
        specs.append(pl.BlockSpec((1, wb // dil, 1, 2, H_B, DH_B), lambda i: (i, 0, 0, 0, 0, 0)))
    return pl.pallas_call(
        _attn_sample_body, grid=(n,),
        in_specs=[pl.BlockSpec((1, 3 * N_GROUPS, H_B, DH_B), lambda i: (i, 0, 0, 0))] + specs
                 + [pl.BlockSpec((N_GROUPS, Q_BLOCK, H_B, 1), lambda i: (0, 0, 0, 0)),
                    pl.BlockSpec((N_GROUPS, H_B, 1), lambda i: (0, 0, 0))],
        out_specs=pl.BlockSpec((1, H_B, DH_B), lambda i: (i, 0, 0)),
        out_shape=jax.ShapeDtypeStruct((n, H_B, DH_B), BF16),
        compiler_params=_cparams(("parallel",)),
        name="dilated_attn_sample",
    )(p_b_s, *views, bias_cache, bias_new)


CACHE_SHIFT_ROWS = 512


def _cache_shift_body(cur_ref, nxt_ref, new_ref, o_ref):
    i = pl.program_id(1)
    t = cur_ref.shape[1]
    o_ref[0, 0:t - 1] = cur_ref[0, 1:t]

    @pl.when(i < pl.num_programs(1) - 1)
    def _():
        o_ref[0, t - 1:t] = nxt_ref[0]

    @pl.when(i == pl.num_programs(1) - 1)
    def _():
        o_ref[0, t - 1:t] = new_ref[0]


def _cache_shift(caches, new_rows):
    outs = []
    for g, (c, new) in enumerate(zip(caches, new_rows)):
        n, wb = c.shape[0], c.shape[1]
        t = min(wb, CACHE_SHIFT_ROWS)
        tail = c.shape[2:]
        zeros = (0,) * len(tail)
        outs.append(pl.pallas_call(
            _cache_shift_body, grid=(n, wb // t),
            in_specs=[pl.BlockSpec((1, t) + tail, lambda b, i: (b, i) + zeros),
                      pl.BlockSpec((1, 1) + tail, lambda b, i: (b, jnp.minimum((i + 1) * t, wb - 1)) + zeros),
                      pl.BlockSpec((1, 1) + tail, lambda b, i: (b, 0) + zeros)],
            out_specs=pl.BlockSpec((1, t) + tail, lambda b, i: (b, i) + zeros),
            out_shape=jax.ShapeDtypeStruct(c.shape, c.dtype),
            compiler_params=_cparams(("parallel", "arbitrary")),
            name=f"kv_cache_shift_g{g}",
        )(c, c, new))
    return outs


def _branch_body(ya_ref, yb_ref, wa_ref, wb_ref, ga_ref, gb_ref, o_ref):
    ba = jnp.dot(ya_ref[...], wa_ref[...].astype(BF16), preferred_element_type=F32)
    bb = jnp.dot(yb_ref[...], wb_ref[...].astype(BF16), preferred_element_type=F32)
    o_ref[...] = (ga_ref[...] * ba + gb_ref[...] * bb).astype(o_ref.dtype)


def _branch_merge(ya, yb, w_pa, w_pb, gates):
    rows = ya.shape[0]
    tm, tn = ROW_TILE, 512
    nj = D_MODEL // tn
    return pl.pallas_call(
        _branch_body, grid=(rows // tm, nj),
        in_specs=[pl.BlockSpec((tm, ya.shape[1]), lambda i, j: (i, 0)),
                  pl.BlockSpec((tm, yb.shape[1]), lambda i, j: (i, 0)),
                  pl.BlockSpec((w_pa.shape[0], tn), lambda i, j: (0, j)),
                  pl.BlockSpec((w_pb.shape[0], tn), lambda i, j: (0, j)),
                  pl.BlockSpec((tm, tn), lambda i, j: (i, j)),
                  pl.BlockSpec((tm, tn), lambda i, j: (i, nj + j))],
        out_specs=pl.BlockSpec((tm, tn), lambda i, j: (i, j)),
        out_shape=jax.ShapeDtypeStruct((rows, D_MODEL), BF16),
        compiler_params=_cparams(("parallel", "parallel")),
        name="branch_proj_gate",
    )(ya, yb, w_pa, w_pb, gates, gates)


def _ln_body(x_ref, r_ref, g_ref, b_ref, *out_refs, alpha):
    v = alpha * x_ref[...] + r_ref[...]
    mu = jnp.mean(v, axis=-1, keepdims=True)
    var = jnp.mean(jnp.square(v - mu), axis=-1, keepdims=True)
    y = (v - mu) * lax.rsqrt(var + LN_EPS) * g_ref[...] + b_ref[...]
    for o_ref in out_refs:
        o_ref[...] = y.astype(o_ref.dtype)


def _residual_ln(x, r, g, b, alpha, out_dtypes, name):
    rows, d = x.shape
    tt = 344
    spec = pl.BlockSpec((tt, d), lambda i: (i, 0))
    vec = pl.BlockSpec((1, d), lambda i: (0, 0))
    return pl.pallas_call(
        functools.partial(_ln_body, alpha=alpha), grid=(rows // tt,),
        in_specs=[spec, spec, vec, vec],
        out_specs=[spec] * len(out_dtypes),
        out_shape=[jax.ShapeDtypeStruct((rows, d), dt) for dt in out_dtypes],
        compiler_params=_cparams(("parallel",)),
        name=name,
    )(x, r, g.reshape(1, d), b.reshape(1, d))


def _router_body(h_ref, w_ref, b_ref, idx_ref, gate_ref):
    logits = _dot(h_ref[...], w_ref[...]) + b_ref[...]
    lane = lax.broadcasted_iota(jnp.int32, logits.shape, 1)
    vals, idxs = [], []
    for _ in range(TOP_K):
        mx = jnp.max(logits, axis=-1, keepdims=True)
        idx = jnp.min(jnp.where(logits == mx, lane, LANES), axis=-1, keepdims=True)
        vals.append(mx)
        idxs.append(idx)
        logits = jnp.where(lane == idx, -jnp.inf, logits)
    es = [jnp.exp(v - vals[0]) for v in vals]
    den = es[0] + es[1] + es[2] + es[3]
    idx_out = jnp.zeros(logits.shape, jnp.int32)
    gate_out = jnp.zeros(logits.shape, F32)
    for k in range(TOP_K):
        idx_out = jnp.where(lane == k, idxs[k], idx_out)
        gate_out = jnp.where(lane == k, es[k] / den, gate_out)
    idx_ref[...] = idx_out
    gate_ref[...] = gate_out


def _router(h, router_w_l, router_b_l):
    rows, d = h.shape
    tt = 344
    w = jnp.pad(router_w_l, ((0, 0), (0, LANES - N_EXPERTS)))
    b = jnp.concatenate([router_b_l, jnp.full((LANES - N_EXPERTS,), -jnp.inf, F32)]).reshape(1, LANES)
    spec = pl.BlockSpec((tt, LANES), lambda i: (i, 0))
    return pl.pallas_call(
        _router_body, grid=(rows // tt,),
        in_specs=[pl.BlockSpec((tt, d), lambda i: (i, 0)),
                  pl.BlockSpec((d, LANES), lambda i: (0, 0)),
                  pl.BlockSpec((1, LANES), lambda i: (0, 0))],
        out_specs=[spec, spec],
        out_shape=[jax.ShapeDtypeStruct((rows, LANES), jnp.int32),
                   jax.ShapeDtypeStruct((rows, LANES), F32)],
        compiler_params=_cparams(("parallel",)),
        name="router_top4",
    )(h, w, b)


def _expert_weight_copy(w_hbm, wbuf, sem, expert, col, slot, part):
    tn = wbuf.shape[-1]
    return pltpu.make_async_copy(w_hbm.at[expert, :, pl.ds(col, tn)], wbuf.at[slot, part], sem.at[slot, part])


def _expert_weights_step(sched_ref, nseg_ref, w_hbm, wbuf, sem, col_of):
    j, i = pl.program_id(0), pl.program_id(1)
    nj = pl.num_programs(0)
    nseg = nseg_ref[0]
    n_parts = wbuf.shape[1]
    seg = sched_ref[3, i]
    slot = (j * nseg + seg) % 2

    @pl.when(sched_ref[1, i] == 1)
    def _():
        is_last_seg = seg == nseg - 1

        @pl.when((j == 0) & (i == 0))
        def _():
            for part in range(n_parts):
                _expert_weight_copy(w_hbm, wbuf, sem, sched_ref[0, 0], col_of(0, part), 0, part).start()

        for part in range(n_parts):
            _expert_weight_copy(w_hbm, wbuf, sem, 0, 0, slot, part).wait()

        @pl.when(jnp.logical_not(is_last_seg & (j == nj - 1)))
        def _():
            nxt_j = jnp.where(is_last_seg, j + 1, j)
            for part in range(n_parts):
                _expert_weight_copy(w_hbm, wbuf, sem, sched_ref[4, i], col_of(nxt_j, part), 1 - slot, part).start()

    return slot


def _gmm1_body(sched_ref, nseg_ref, x_ref, w_hbm, bg_ref, bl_ref, a_ref, wbuf, sem):
    i = pl.program_id(1)
    tn = a_ref.shape[1]
    slot = _expert_weights_step(sched_ref, nseg_ref, w_hbm, wbuf, sem,
                                lambda j, part: pl.multiple_of(part * D_FF + j * tn, LANES))
    for r in range(MOE_BLK // MOE_SUB + 1):
        @pl.when(sched_ref[2, i] == r)
        def _():
            used = r * MOE_SUB
            if used:
                x = x_ref[0:used, :]
                hg = jnp.dot(x, wbuf[slot, 0].astype(BF16), preferred_element_type=F32) + bg_ref[0]
                hl = jnp.dot(x, wbuf[slot, 1].astype(BF16), preferred_element_type=F32) + bl_ref[0]
                glu = jnp.minimum(hg, SWIGLU_LIMIT)
                lin = jnp.clip(hl, -SWIGLU_LIMIT, SWIGLU_LIMIT)
                a_ref[0:used, :] = (glu * jax.nn.sigmoid(SWIGLU_ALPHA * glu) * (lin + 1.0)).astype(a_ref.dtype)
            if used < MOE_BLK:
                a_ref[used:MOE_BLK, :] = jnp.zeros((MOE_BLK - used, a_ref.shape[1]), a_ref.dtype)


def _gmm_scratch(k, tn, n_parts):
    return [pltpu.VMEM((2, n_parts, k, tn), F32), pltpu.SemaphoreType.DMA((2, n_parts))]


def _gmm1(xs, w_gu_l, b_gu_l, sched, nseg):
    n_slots, d = xs.shape
    nb = n_slots // MOE_BLK
    tn = min(512, D_FF)
    nj = D_FF // tn
    b3 = b_gu_l.reshape(N_EXPERTS, 1, 2 * D_FF)
    grid_spec = pltpu.PrefetchScalarGridSpec(
        num_scalar_prefetch=2, grid=(nj, nb),
        in_specs=[pl.BlockSpec((MOE_BLK, d), lambda j, i, sc, ns: (sc[5, i], 0)),
                  pl.BlockSpec(memory_space=pl.ANY),
                  pl.BlockSpec((1, 1, tn), lambda j, i, sc, ns: (sc[0, i], 0, j)),
                  pl.BlockSpec((1, 1, tn), lambda j, i, sc, ns: (sc[0, i], 0, nj + j))],
        out_specs=pl.BlockSpec((MOE_BLK, tn), lambda j, i, sc, ns: (i, j)),
        scratch_shapes=_gmm_scratch(d, tn, 2))
    return pl.pallas_call(
        _gmm1_body, grid_spec=grid_spec,
        out_shape=jax.ShapeDtypeStruct((n_slots, D_FF), BF16),
        compiler_params=_cparams(("arbitrary", "arbitrary")),
        name="moe_up_swiglu",
    )(sched, nseg, xs, w_gu_l, b3, b3)


def _gmm2_body(sched_ref, nseg_ref, a_ref, w_hbm, b_ref, y_ref, wbuf, sem):
    i = pl.program_id(1)
    tn = y_ref.shape[1]
    slot = _expert_weights_step(sched_ref, nseg_ref, w_hbm, wbuf, sem,
                                lambda j, part: pl.multiple_of(j * tn, LANES))
    for r in range(MOE_BLK // MOE_SUB + 1):
        @pl.when(sched_ref[2, i] == r)
        def _():
            used = r * MOE_SUB
            if used:
                y_ref[0:used, :] = (jnp.dot(a_ref[0:used, :], wbuf[slot, 0].astype(BF16),
                                            preferred_element_type=F32) + b_ref[0])
            if used < MOE_BLK:
                y_ref[used:MOE_BLK, :] = jnp.zeros((MOE_BLK - used, y_ref.shape[1]), y_ref.dtype)


def _gmm2(a, w_dn_l, b_dn_l, sched, nseg):
    n_slots, f = a.shape
    nb = n_slots // MOE_BLK
    tn = min(1024, D_MODEL)
    nj = D_MODEL // tn
    b3 = b_dn_l.reshape(N_EXPERTS, 1, D_MODEL)
    grid_spec = pltpu.PrefetchScalarGridSpec(
        num_scalar_prefetch=2, grid=(nj, nb),
        in_specs=[pl.BlockSpec((MOE_BLK, f), lambda j, i, sc, ns: (sc[5, i], 0)),
                  pl.BlockSpec(memory_space=pl.ANY),
                  pl.BlockSpec((1, 1, tn), lambda j, i, sc, ns: (sc[0, i], 0, j))],
        out_specs=pl.BlockSpec((MOE_BLK, tn), lambda j, i, sc, ns: (i, j)),
        scratch_shapes=_gmm_scratch(f, tn, 1))
    return pl.pallas_call(
        _gmm2_body, grid_spec=grid_spec,
        out_shape=jax.ShapeDtypeStruct((n_slots, D_MODEL), F32),
        compiler_params=_cparams(("arbitrary", "arbitrary")),
        name="moe_down",
    )(sched, nseg, a, w_dn_l, b3)


DMA_ISSUE_UNROLL = 8


def _moe_gather_body(nsub_ref, tok_ref, h_hbm, o_ref, buf, sem):
    used = nsub_ref[pl.program_id(0)] * MOE_SUB

    for s in range(MOE_BLK // MOE_SUB):
        def start_row(r, carry, s=s):
            pltpu.make_async_copy(h_hbm.at[pl.ds(tok_ref[0, 0, r], 1)], buf.at[pl.ds(r, 1)], sem.at[s]).start()
            return carry

        @pl.when(s * MOE_SUB < used)
        def _():
            lax.fori_loop(s * MOE_SUB, (s + 1) * MOE_SUB, start_row, 0, unroll=DMA_ISSUE_UNROLL)

    for s in range(MOE_BLK // MOE_SUB):
        rows = slice(s * MOE_SUB, (s + 1) * MOE_SUB)

        @pl.when(s * MOE_SUB < used)
        def _():
            pltpu.make_async_copy(h_hbm.at[pl.ds(0, MOE_SUB)], buf.at[rows], sem.at[s]).wait()
            o_ref[rows, :] = buf[rows, :].astype(o_ref.dtype)

        @pl.when(s * MOE_SUB >= used)
        def _():
            o_ref[rows, :] = jnp.zeros((MOE_SUB, o_ref.shape[1]), o_ref.dtype)


def _moe_gather(h32, slot_tok, nsub):
    d = h32.shape[1]
    nb = nsub.shape[0]
    grid_spec = pltpu.PrefetchScalarGridSpec(
        num_scalar_prefetch=1, grid=(nb,),
        in_specs=[pl.BlockSpec((1, 1, MOE_BLK), lambda i, ns: (i, 0, 0), memory_space=pltpu.SMEM),
                  pl.BlockSpec(memory_space=pl.ANY)],
        out_specs=pl.BlockSpec((MOE_BLK, d), lambda i, ns: (i, 0)),
        scratch_shapes=[pltpu.VMEM((MOE_BLK, d), F32), pltpu.SemaphoreType.DMA((MOE_BLK // MOE_SUB,))])
    return pl.pallas_call(
        _moe_gather_body, grid_spec=grid_spec,
        out_shape=jax.ShapeDtypeStruct((nb * MOE_BLK, d), BF16),
        compiler_params=_cparams(("arbitrary",)),
        name="moe_gather_rows",
    )(nsub, slot_tok.reshape(nb, 1, MOE_BLK), h32)


COMBINE_TILE = 192


def _combine_row_copy(y_hbm, ybuf, sem, slot, t, k):
    return pltpu.make_async_copy(y_hbm.at[pl.ds(slot, 1)], ybuf.at[k, pl.ds(t, 1)], sem.at[k])


def _combine_ln_body(dest_ref, y_hbm, gate_ref, h_ref, g_ref, b_ref, o_ref, ybuf, sem, *, alpha):
    tt = h_ref.shape[0]

    def start_rows(t, carry):
        for k in range(TOP_K):
            _combine_row_copy(y_hbm, ybuf, sem, dest_ref[0, 0, t * TOP_K + k], t, k).start()
        return carry

    lax.fori_loop(0, tt, start_rows, 0, unroll=DMA_ISSUE_UNROLL)
    for k in range(TOP_K):
        pltpu.make_async_copy(y_hbm.at[pl.ds(0, tt)], ybuf.at[k], sem.at[k]).wait()
    gate = gate_ref[...]
    f = gate[:, 0:1] * ybuf[0]
    for k in range(1, TOP_K):
        f = f + gate[:, k:k + 1] * ybuf[k]
    v = alpha * h_ref[...] + f
    mu = jnp.mean(v, axis=-1, keepdims=True)
    var = jnp.mean(jnp.square(v - mu), axis=-1, keepdims=True)
    o_ref[...] = (v - mu) * lax.rsqrt(var + LN_EPS) * g_ref[...] + b_ref[...]


def _combine_ln(h32, y, dest, gate_pad, g, b, alpha):
    rows, d = h32.shape
    tt = COMBINE_TILE
    nt = rows // tt
    row = pl.BlockSpec((tt, d), lambda i: (i, 0))
    vec = pl.BlockSpec((1, d), lambda i: (0, 0))
    return pl.pallas_call(
        functools.partial(_combine_ln_body, alpha=alpha), grid=(nt,),
        in_specs=[pl.BlockSpec((1, 1, tt * TOP_K), lambda i: (i, 0, 0), memory_space=pltpu.SMEM),
                  pl.BlockSpec(memory_space=pl.ANY),
                  pl.BlockSpec((tt, LANES), lambda i: (i, 0)), row, vec, vec],
        out_specs=row,
        out_shape=jax.ShapeDtypeStruct((rows, d), F32),
        scratch_shapes=[pltpu.VMEM((TOP_K, tt, d), F32), pltpu.SemaphoreType.DMA((TOP_K,))],
        compiler_params=_cparams(("arbitrary",)),
        name="moe_combine_ln2",
    )(dest.reshape(nt, 1, tt * TOP_K), y, gate_pad, h32, g.reshape(1, d), b.reshape(1, d))


def _moe(h32, n_tok, router_w_l, router_b_l, w_gu_l, b_gu_l, w_dn_l, b_dn_l):
    rows = h32.shape[0]
    idx_pad, gate_pad = _router(h32, router_w_l, router_b_l)
    top_idx = idx_pad[:n_tok, :TOP_K]
    n_assign = n_tok * TOP_K
    n_blocks = -(-n_assign // MOE_BLK) + N_EXPERTS
    n_slots = n_blocks * MOE_BLK
    flat_e = top_idx.reshape(-1)
    onehot = (flat_e[:, None] == jnp.arange(N_EXPERTS, dtype=jnp.int32)[None, :]).astype(jnp.int32)
    counts = jnp.sum(onehot, axis=0)
    padded = (counts + MOE_BLK - 1) // MOE_BLK * MOE_BLK
    pad_end = jnp.cumsum(padded)
    pad_start = pad_end - padded
    dest = jnp.sum((jnp.cumsum(onehot, axis=0) - onehot + pad_start[None, :]) * onehot, axis=1)
    slot_tok = jnp.zeros((n_slots,), jnp.int32).at[dest].set(jnp.arange(n_assign, dtype=jnp.int32) // TOP_K)
    n_used = pad_end[-1] // MOE_BLK
    blk = jnp.minimum(jnp.arange(n_blocks, dtype=jnp.int32), n_used - 1)
    block_e = jnp.minimum(jnp.sum((pad_end[None, :] <= (blk * MOE_BLK)[:, None]).astype(jnp.int32), axis=1),
                          N_EXPERTS - 1)
    first = jnp.concatenate([jnp.ones((1,), jnp.int32), (block_e[1:] != block_e[:-1]).astype(jnp.int32)])
    seg_rows = jnp.clip(counts[block_e] - (blk * MOE_BLK - pad_start[block_e]), 0, MOE_BLK)
    nsub = jnp.where(jnp.arange(n_blocks) < n_used, (seg_rows + MOE_SUB - 1) // MOE_SUB, 0).astype(jnp.int32)
    seg_idx = jnp.cumsum(first) - 1
    nseg = seg_idx[-1] + 1
    seg_expert = jnp.zeros((n_blocks + 1,), jnp.int32).at[jnp.where(first == 1, seg_idx, n_blocks)].set(block_e)
    next_expert = jnp.where(seg_idx + 1 < nseg, seg_expert[jnp.minimum(seg_idx + 1, n_blocks - 1)], block_e[0])
    sched = jnp.stack([block_e, first, nsub, seg_idx, next_expert, blk]).astype(jnp.int32)
    xs = _moe_gather(h32, slot_tok, nsub)
    a = _gmm1(xs, w_gu_l, b_gu_l, sched, nseg.reshape(1).astype(jnp.int32))
    y = _gmm2(a, w_dn_l, b_dn_l, sched, nseg.reshape(1).astype(jnp.int32))
    dest_pad = jnp.pad(dest.reshape(n_tok, TOP_K), ((0, rows - n_tok), (0, 0)))
    row_is_token = (jnp.arange(rows) < n_tok)[:, None]
    return y, dest_pad, jnp.where(row_is_token, gate_pad, 0.0)


def _attention_bias_tables(rel_bias):
    qi = jnp.arange(Q_BLOCK)[:, None]
    kj = jnp.arange(2 * Q_BLOCK)[None, :]
    rel = qi + Q_BLOCK - kj
    prompt_tabs, cache_tabs, new_tabs = [], [], []
    buckets = jnp.arange(NUM_BUCKETS, dtype=jnp.int32)
    for gi, (win, dil) in enumerate(GROUPS):
        steps = win // dil
        bias_g = rel_bias[:, gi * H_B:(gi + 1) * H_B].astype(F32)
        hit = _t5_bucket(jnp.maximum(rel, 0) * dil)[None, :, :, None] == buckets
        tab = jnp.sum(jnp.where(hit, bias_g.T[:, None, None, :], 0.0), axis=-1)
        prompt_tabs.append(jnp.where(((rel >= 0) & (rel <= steps))[None], tab, -jnp.inf))
        bj = bias_g[_t5_bucket(jnp.arange(steps + 1) * dil)]
        cache_tabs.append(bj[:0:-1, :, None])
        new_tabs.append(bj[0, :, None])
    return prompt_tabs, jnp.stack(cache_tabs), jnp.stack(new_tabs)


def kernel(x_prompt, x_sample, cache_kv_w128, cache_kv_w512, cache_kv_w2048, state_conv, state_delta, rel_bias,
           w_in, b_gate, conv_w, a_log, dt_bias, o_norm_w, w_branch_a, w_branch_b, w_out,
           ln1_g, ln1_b, router_w, router_b, w_gu, b_gu, w_dn, b_dn, ln2_g, ln2_b):
    batch, seq, d = x_prompt.shape
    n_s = x_sample.shape[0]
    n_p = batch * seq
    n_tok = n_p + n_s
    rows = 8 * ROW_TILE
    depth = w_in.shape[0]
    alpha = (2.0 * depth) ** 0.25
    kv_caches = (cache_kv_w128, cache_kv_w512, cache_kv_w2048)
    hw = H_B * DH_B

    x = jnp.concatenate([x_prompt.reshape(n_p, d), x_sample.reshape(n_s, d),
                         jnp.zeros((rows - n_tok, d), F32)], axis=0)
    prompt_tabs, bias_cache, bias_new = _attention_bias_tables(rel_bias)

    kv_p = [[] for _ in GROUPS]
    kv_s = [[] for _ in GROUPS]
    conv_p, conv_s, delta_p, delta_s = [], [], [], []
    for l in range(depth):
        xb = x.astype(BF16)
        w_in_t = jnp.swapaxes(w_in, 1, 2)
        in_proj = functools.partial(_mm, xb, w_in_t, layer=l, tm=ROW_TILE, w_is_nk=True)
        p_a = in_proj(n_cols=OFF_BETA, tn=512, name="in_proj_deltanet")
        p_ba = in_proj(n_cols=LANES, col0=OFF_BETA, tn=LANES, name="in_proj_beta_alpha")
        p_b = in_proj(n_cols=N_GROUPS * QKV_B, col0=OFF_B, tn=512, name="in_proj_attention")
        gates = in_proj(n_cols=2 * D_MODEL, col0=OFF_GATE, tn=512,
                        bias=b_gate[l].reshape(1, 2 * D_MODEL), name="in_proj_gates")

        beta, g, gc = _beta_g(p_ba, a_log[l], dt_bias[l])
        qkv = _conv_qkv_prompt(p_a, conv_w[l], n_p, seq)
        ya_p, s_p = _gdn_prompt(qkv, p_a, beta[:n_p], gc[:n_p], o_norm_w[l], batch, seq)
        ya_s, cs, s_s = _gdn_sample(p_a[n_p:n_tok].reshape(n_s, 1, OFF_BETA), state_conv[l], conv_w[l],
                                    beta[n_p:n_tok], g[n_p:n_tok], state_delta[l], o_norm_w[l])
        ya = jnp.concatenate([ya_p, ya_s.reshape(n_s, H_A * DV_A),
                              jnp.zeros((rows - n_tok, H_A * DV_A), BF16)], axis=0)

        outs, lses = [], []
        for gi in range(N_GROUPS):
            o_g, lse_g = _attn_prompt(p_b, gi, prompt_tabs[gi], batch, seq)
            outs.append(o_g)
            lses.append(lse_g)
        yb_p = _merge_groups(outs, lses)
        p_b_s = p_b[n_p:n_tok]
        yb_s = _attn_sample(p_b_s.reshape(n_s, 3 * N_GROUPS, H_B, DH_B), [c[l] for c in kv_caches],
                            bias_cache, bias_new)
        yb = jnp.concatenate([yb_p, yb_s.reshape(n_s, hw), jnp.zeros((rows - n_tok, hw), BF16)], axis=0)

        merged = _branch_merge(ya, yb, w_branch_a[l], w_branch_b[l], gates)
        mix = _mm(merged, w_out, layer=l, n_cols=D_MODEL, tm=ROW_TILE, tn=512, name="out_proj")
        (h32,) = _residual_ln(x, mix, ln1_g[l], ln1_b[l], alpha, (F32,), "deepnorm_ln1")
        y_slots, dest, gate = _moe(h32, n_tok, router_w[l], router_b[l], w_gu[l], b_gu[l], w_dn[l], b_dn[l])
        x = _combine_ln(h32, y_slots, dest, gate, ln2_g[l], ln2_b[l], alpha)

        new_rows = []
        for gi, (win, dil) in enumerate(GROUPS):
            keep = min(win, seq)
            c0, c1 = gi * QKV_B + hw, (gi + 1) * QKV_B
            last = [p_b[(b + 1) * seq - keep:(b + 1) * seq, c0:c1] for b in range(batch)]
            kv_p[gi].append(jnp.stack(last).reshape(batch, keep, 2, H_B, DH_B))
            new_rows.append(p_b_s[:, c0:c1].reshape(n_s, 1, 2, H_B, DH_B))
        for gi, shifted in enumerate(_cache_shift([c[l] for c in kv_caches], new_rows)):
            kv_s[gi].append(shifted)
        conv_p.append(jnp.stack([p_a[(b + 1) * seq - (CONV_W - 1):(b + 1) * seq, :CONV_DIM] for b in range(batch)]))
        conv_s.append(cs)
        delta_p.append(s_p)
        delta_s.append(s_s)

    return (x[:n_p].reshape(batch, seq, d), x[n_p:n_tok].reshape(n_s, 1, d),
            jnp.stack(kv_p[0]), jnp.stack(kv_s[0]),
            jnp.stack(kv_p[1]), jnp.stack(kv_s[1]),
            jnp.stack(kv_p[2]), jnp.stack(kv_s[2]),
            jnp.stack(conv_p), jnp.stack(conv_s),
            jnp.stack(delta_p), jnp.stack(delta_s))
```
